```python
import math
import jax
import jax.numpy as jnp
from jax import lax
import numpy as np

D_MODEL = 2048
BATCH = 8
SEQ = 4096
DEPTH = 4

CTX_LEN = 256
GRID_W = 64
N_MIXERS = 3
HEAD_DIM = 64
N_HEADS = D_MODEL // HEAD_DIM
N_KV_HEADS = N_HEADS // 8
GQA_GROUP = N_HEADS // N_KV_HEADS
N_DIFF_HEADS = N_HEADS // 2
NA_ROWS = 8
NA_COLS = 16
NA_QCOLS = 16
NA_KCOLS = 32
SW_WINDOW = 128
Q_BLOCK = 128
ROPE_BASE = 10000.0
N_EXPERTS = 32
TOP_K = 4
D_EXPERT = D_MODEL // 4
SWIGLU_LIMIT = 7.0
SWIGLU_ALPHA = 1.702
NORM_EPS = 1e-6
NEG_INF = -1e30

kernel_name = 'hybrid_na_swa_diffattn_moe_prefix_trunk'


def rmsnorm(x, g, eps=NORM_EPS):
    x32 = x.astype(jnp.float32)
    y = x32 * lax.rsqrt(jnp.mean(x32 * x32, axis=-1, keepdims=True) + eps)
    return (y * g.astype(jnp.float32)).astype(x.dtype)


def modulate(h, shift, scale):
    return h * (1.0 + scale) + shift


def softmax32(s):
    return jax.nn.softmax(s.astype(jnp.float32), axis=-1)


def axial_rope(x):
    n = x.shape[1]
    half = x.shape[-1] // 2
    nf = half // 2
    t = jnp.arange(n)
    inv_freq = ROPE_BASE ** (-jnp.arange(nf, dtype=jnp.float32) / nf)

    def rotate(xa, pos):
        ang = pos.astype(jnp.float32)[:, None] * inv_freq[None, :]
        cos = jnp.cos(ang)[None, :, None, :]
        sin = jnp.sin(ang)[None, :, None, :]
        xa = xa.astype(jnp.float32)
        x1, x2 = xa[..., :nf], xa[..., nf:]
        return jnp.concatenate([x1 * cos - x2 * sin, x2 * cos + x1 * sin], axis=-1)

    out = jnp.concatenate([rotate(x[..., :half], t // GRID_W), rotate(x[..., half:], t % GRID_W)], axis=-1)
    return out.astype(x.dtype)


def neighbourhood_attention(hc, hl, w_qkv, rpb, w_o, ctx_out):
    bsz, n_lat, _ = hl.shape
    n_ctx = hc.shape[1]
    rows = n_lat // GRID_W
    kr = min(NA_ROWS, rows)
    qkv = (jnp.concatenate([hc, hl], axis=1) @ w_qkv).reshape(bsz, n_ctx + n_lat, 3, N_HEADS, HEAD_DIM)
    q = qkv[:, :, 0] * HEAD_DIM ** -0.5
    k = qkv[:, :, 1]
    v = qkv[:, :, 2]
    qc, kc, vc = q[:, :n_ctx], k[:, :n_ctx], v[:, :n_ctx]
    oc = None
    if ctx_out:
        pc = softmax32(jnp.einsum('bqhd,bkhd->bhqk', qc, kc))
        oc = jnp.einsum('bhqk,bkhd->bqhd', pc.astype(vc.dtype), vc).reshape(bsz, n_ctx, -1) @ w_o
    qg = q[:, n_ctx:].reshape(bsz, rows, GRID_W, N_HEADS, HEAD_DIM)
    kg = k[:, n_ctx:].reshape(bsz, rows, GRID_W, N_HEADS, HEAD_DIM)
    vg = v[:, n_ctx:].reshape(bsz, rows, GRID_W, N_HEADS, HEAD_DIM)
    n_cb = GRID_W // NA_QCOLS
    row_start = jnp.clip(jnp.arange(rows) - NA_ROWS // 2, 0, rows - kr)
    qcol = jnp.arange(GRID_W).reshape(n_cb, NA_QCOLS)
    win_start = jnp.clip(qcol - NA_COLS // 2, 0, GRID_W - NA_COLS)
    blk_start = jnp.clip(jnp.arange(n_cb) * NA_QCOLS - NA_COLS // 2, 0, GRID_W - NA_KCOLS)
    kcol = blk_start[:, None] + jnp.arange(NA_KCOLS)
    col_ok = (kcol[:, None, :] >= win_start[:, :, None]) & (kcol[:, None, :] < win_start[:, :, None] + NA_COLS)
    col_off = jnp.clip(kcol[:, None, :] - qcol[:, :, None] + NA_COLS - 1, 0, 2 * NA_COLS - 2)
    rpb_cols = rpb[:, :, col_off]
    n_loc = kr * NA_KCOLS
    mask = jnp.broadcast_to(col_ok[:, :, None, :], (n_cb, NA_QCOLS, kr, NA_KCOLS)).reshape(n_cb, NA_QCOLS, n_loc)

    def row_block(r):
        rs = row_start[r]
        kb = lax.dynamic_slice_in_dim(kg, rs, kr, axis=1)[:, :, kcol]
        vb = lax.dynamic_slice_in_dim(vg, rs, kr, axis=1)[:, :, kcol]
        kb = kb.transpose(0, 2, 1, 3, 4, 5).reshape(bsz, n_cb, n_loc, N_HEADS, HEAD_DIM)
        vb = vb.transpose(0, 2, 1, 3, 4, 5).reshape(bsz, n_cb, n_loc, N_HEADS, HEAD_DIM)
        qb = lax.dynamic_index_in_dim(qg, r, axis=1, keepdims=False).reshape(bsz, n_cb, NA_QCOLS, N_HEADS, HEAD_DIM)
        row_off = rs + jnp.arange(kr) - r + NA_ROWS - 1
        bias = jnp.take(rpb_cols, row_off, axis=1)
        bias = bias.transpose(0, 2, 3, 1, 4).reshape(N_HEADS, n_cb, NA_QCOLS, n_loc)
        s_loc = jnp.einsum('bnqhd,bnkhd->bhnqk', qb, kb).astype(jnp.float32) + bias.astype(jnp.float32)
        s_loc = jnp.where(mask, s_loc, NEG_INF)
        s_ctx = jnp.einsum('bnqhd,bkhd->bhnqk', qb, kc).astype(jnp.float32)
        p = softmax32(jnp.concatenate([s_loc, s_ctx], axis=-1)).astype(vb.dtype)
        o = (jnp.einsum('bhnqk,bnkhd->bnqhd', p[..., :n_loc], vb)
             + jnp.einsum('bhnqk,bkhd->bnqhd', p[..., n_loc:], vc))
        return o.reshape(bsz, GRID_W, N_HEADS * HEAD_DIM)

    ol = lax.map(row_block, jnp.arange(rows))
    ol = ol.transpose(1, 0, 2, 3).reshape(bsz, n_lat, -1) @ w_o
    return ol, oc


def sliding_window_gqa(hc, hl, w_qkv, sinks, w_o, ctx_out):
    bsz, n_lat, _ = hl.shape
    n_ctx = hc.shape[1]
    n_tok = n_ctx + n_lat
    nq = N_HEADS * HEAD_DIM
    nkv = N_KV_HEADS * HEAD_DIM
    scale = HEAD_DIM ** -0.5
    qkv = jnp.concatenate([hc, hl], axis=1) @ w_qkv
    q = qkv[..., :nq].reshape(bsz, n_tok, N_HEADS, HEAD_DIM)
    k = qkv[..., nq:nq + nkv].reshape(bsz, n_tok, N_KV_HEADS, HEAD_DIM)
    v = qkv[..., nq + nkv:].reshape(bsz, n_tok, N_KV_HEADS, HEAD_DIM)
    kc, vc = k[:, :n_ctx], v[:, :n_ctx]
    ql = (axial_rope(q[:, n_ctx:]) * scale).reshape(bsz, n_lat, N_KV_HEADS, GQA_GROUP, HEAD_DIM)
    kl = axial_rope(k[:, n_ctx:])
    vl = v[:, n_ctx:]
    sink = sinks.astype(jnp.float32).reshape(N_KV_HEADS, GQA_GROUP, 1, 1)

    def sink_softmax(s):
        s_full = jnp.concatenate([s, jnp.broadcast_to(sink, s.shape[:-1] + (1,))], axis=-1)
        return softmax32(s_full)[..., :-1]

    oc = None
    if ctx_out:
        qc = (q[:, :n_ctx] * scale).reshape(bsz, n_ctx, N_KV_HEADS, GQA_GROUP, HEAD_DIM)
        pc = sink_softmax(jnp.einsum('bqhgd,bkhd->bhgqk', qc, kc).astype(jnp.float32))
        oc = jnp.einsum('bhgqk,bkhd->bqhgd', pc.astype(vc.dtype), vc).reshape(bsz, n_ctx, -1) @ w_o
    n_blk = n_lat // Q_BLOCK
    pad = jnp.zeros((bsz, Q_BLOCK, N_KV_HEADS, HEAD_DIM), kl.dtype)
    kp = jnp.concatenate([pad, kl, pad], axis=1)
    vp = jnp.concatenate([pad, vl, pad], axis=1)
    n_band = 3 * Q_BLOCK

    def band_block(n):
        q0 = n * Q_BLOCK
        qb = lax.dynamic_slice_in_dim(ql, q0, Q_BLOCK, axis=1)
        kb = lax.dynamic_slice_in_dim(kp, q0, n_band, axis=1)
        vb = lax.dynamic_slice_in_dim(vp, q0, n_band, axis=1)
        qpos = q0 + jnp.arange(Q_BLOCK)
        kpos = q0 - Q_BLOCK + jnp.arange(n_band)
        ok = (jnp.abs(qpos[:, None] - kpos[None, :]) <= SW_WINDOW) & (kpos >= 0) & (kpos < n_lat)
        s_loc = jnp.where(ok, jnp.einsum('bqhgd,bkhd->bhgqk', qb, kb).astype(jnp.float32), NEG_INF)
        s_ctx = jnp.einsum('bqhgd,bkhd->bhgqk', qb, kc).astype(jnp.float32)
        p = sink_softmax(jnp.concatenate([s_loc, s_ctx], axis=-1)).astype(vb.dtype)
        o = (jnp.einsum('bhgqk,bkhd->bqhgd', p[..., :n_band], vb)
             + jnp.einsum('bhgqk,bkhd->bqhgd', p[..., n_band:], vc))
        return o.reshape(bsz, Q_BLOCK, -1)

    ol = lax.map(band_block, jnp.arange(n_blk))
    ol = ol.transpose(1, 0, 2, 3).reshape(bsz, n_lat, -1) @ w_o
    return ol, oc


def differential_attention(hc, hl, w_qkv, lam_q1, lam_k1, lam_q2, lam_k2, sub_g, w_o, lambda_init, ctx_out):
    bsz, n_lat, _ = hl.shape
    n_ctx = hc.shape[1]
    n_tok = n_ctx + n_lat
    width = N_HEADS * HEAD_DIM
    scale = HEAD_DIM ** -0.5
    qkv = jnp.concatenate([hc, hl], axis=1) @ w_qkv
    q = qkv[..., :width].reshape(bsz, n_tok, 2 * N_DIFF_HEADS, HEAD_DIM)
    k = qkv[..., width:2 * width].reshape(bsz, n_tok, 2 * N_DIFF_HEADS, HEAD_DIM)
    v = qkv[..., 2 * width:].reshape(bsz, n_tok, N_DIFF_HEADS, 2 * HEAD_DIM)
    kc = k[:, :n_ctx].reshape(bsz, n_ctx, N_DIFF_HEADS, 2, HEAD_DIM)
    vc = v[:, :n_ctx]
    ql = (axial_rope(q[:, n_ctx:]) * scale).reshape(bsz, n_lat, N_DIFF_HEADS, 2, HEAD_DIM)
    kl = axial_rope(k[:, n_ctx:]).reshape(bsz, n_lat, N_DIFF_HEADS, 2, HEAD_DIM)
    vl = v[:, n_ctx:]
    lam = (jnp.exp(jnp.sum(lam_q1.astype(jnp.float32) * lam_k1.astype(jnp.float32)))
           - jnp.exp(jnp.sum(lam_q2.astype(jnp.float32) * lam_k2.astype(jnp.float32))) + lambda_init)

    def diff_weights(s):
        p = softmax32(s)
        return p[:, :, 0] - lam * p[:, :, 1]

    def head_out(o):
        o = rmsnorm(o, sub_g, 1e-5) * (1.0 - lambda_init)
        return o.reshape(o.shape[0], o.shape[1], -1) @ w_o

    oc = None
    if ctx_out:
        qc = (q[:, :n_ctx] * scale).reshape(bsz, n_ctx, N_DIFF_HEADS, 2, HEAD_DIM)
        ac = diff_weights(jnp.einsum('bqhid,bkhid->bhiqk', qc, kc).astype(jnp.float32))
        oc = head_out(jnp.einsum('bhqk,bkhe->bqhe', ac.astype(vc.dtype), vc))
    k_all = jnp.concatenate([kl, kc], axis=1)
    v_all = jnp.concatenate([vl, vc], axis=1)
    n_blk = n_lat // Q_BLOCK

    def dense_block(n):
        qb = lax.dynamic_slice_in_dim(ql, n * Q_BLOCK, Q_BLOCK, axis=1)
        a = diff_weights(jnp.einsum('bqhid,bkhid->bhiqk', qb, k_all).astype(jnp.float32))
        return jnp.einsum('bhqk,bkhe->bqhe', a.astype(v_all.dtype), v_all)

    ol = lax.map(dense_block, jnp.arange(n_blk))
    ol = ol.transpose(1, 0, 2, 3, 4).reshape(bsz, n_lat, N_DIFF_HEADS, 2 * HEAD_DIM)
    return head_out(ol), oc


def moe(h, w_router, b_router, w_gu, b_gu, w_down, b_down):
    shape = h.shape
    t = h.reshape(-1, shape[-1])
    logits = (t @ w_router + b_router).astype(jnp.float32)
    top_val, top_idx = lax.top_k(logits, TOP_K)
    gates = jax.nn.softmax(top_val, axis=-1)
    combine = jnp.sum(jax.nn.one_hot(top_idx, N_EXPERTS, dtype=jnp.float32) * gates[..., None], axis=1)

    def expert(acc, p):
        wgu, bgu, wd, bd, cw = p
        gu = t @ wgu + bgu
        gate = jnp.minimum(gu[:, :D_EXPERT], SWIGLU_LIMIT)
        up = jnp.clip(gu[:, D_EXPERT:], -SWIGLU_LIMIT, SWIGLU_LIMIT)
        y = ((up + 1.0) * (gate * jax.nn.sigmoid(SWIGLU_ALPHA * gate))) @ wd + bd
        return acc + cw.astype(y.dtype)[:, None] * y, None

    out, _ = lax.scan(expert, jnp.zeros_like(t), (w_gu, b_gu, w_down, b_down, combine.T))
    return out.reshape(shape)


def setup_inputs(seed: int = 0) -> dict:
    key = jax.random.key(seed)
    ks = iter(jax.random.split(key, 40))
    f32 = jnp.float32
    D = D_MODEL
    n_a = len(range(0, DEPTH, N_MIXERS))
    n_b = len(range(1, DEPTH, N_MIXERS))
    n_c = len(range(2, DEPTH, N_MIXERS))

    def nrm(shape, scale):
        return jax.random.normal(next(ks), shape, f32) * scale

    return {
        'x': nrm((BATCH, SEQ, D), 1.0),
        'c': nrm((BATCH, D), 1.0),
        'ctx': nrm((BATCH, CTX_LEN, D), 1.0),
        'c_ctx': nrm((D,), 1.0),
        'ada_w': nrm((DEPTH, D, 6 * D), 0.5 * D ** -0.5),
        'ada_b': nrm((DEPTH, 6 * D), 0.02),
        'norm_attn_g': 1.0 + nrm((DEPTH, D), 0.1),
        'norm_ffn_g': 1.0 + nrm((DEPTH, D), 0.1),
        'na_w_qkv': nrm((n_a, D, 3 * D), D ** -0.5),
        'na_rpb': nrm((n_a, N_HEADS, 2 * NA_ROWS - 1, 2 * NA_COLS - 1), 0.1),
        'na_w_o': nrm((n_a, D, D), D ** -0.5),
        'sw_w_qkv': nrm((n_b, D, N_HEADS * HEAD_DIM + 2 * N_KV_HEADS * HEAD_DIM), D ** -0.5),
        'sw_sinks': nrm((n_b, N_HEADS), 1.0),
        'sw_w_o': nrm((n_b, D, D), D ** -0.5),
        'diff_w_qkv': nrm((n_c, D, 3 * D), D ** -0.5),
        'diff_lam_q1': nrm((n_c, HEAD_DIM), 0.1),
        'diff_lam_k1': nrm((n_c, HEAD_DIM), 0.1),
        'diff_lam_q2': nrm((n_c, HEAD_DIM), 0.1),
        'diff_lam_k2': nrm((n_c, HEAD_DIM), 0.1),
        'diff_sub_g': 1.0 + nrm((n_c, 2 * HEAD_DIM), 0.1),
        'diff_w_o': nrm((n_c, D, D), D ** -0.5),
        'router_w': nrm((DEPTH, D, N_EXPERTS), D ** -0.5),
        'router_b': nrm((DEPTH, N_EXPERTS), 0.01),
        'exp_w_gu': nrm((DEPTH, N_EXPERTS, D, 2 * D_EXPERT), D ** -0.5),
        'exp_b_gu': nrm((DEPTH, N_EXPERTS, 2 * D_EXPERT), 0.02),
        'exp_w_down': nrm((DEPTH, N_EXPERTS, D_EXPERT, D), D_EXPERT ** -0.5),
        'exp_b_down': nrm((DEPTH, N_EXPERTS, D), 0.02),
        'final_g': 1.0 + nrm((D,), 0.1),
    }


def reference(x, c, ctx, c_ctx, ada_w, ada_b, norm_attn_g, norm_ffn_g, na_w_qkv, na_rpb, na_w_o,
              sw_w_qkv, sw_sinks, sw_w_o, diff_w_qkv, diff_lam_q1, diff_lam_k1, diff_lam_q2,
              diff_lam_k2, diff_sub_g, diff_w_o, router_w, router_b, exp_w_gu, exp_b_gu,
              exp_w_down, exp_b_down, final_g):
    bsz = x.shape[0]
    n_ctx = ctx.shape[1]
    xl, xc = x, ctx
    s_lat = jax.nn.silu(c)
    s_ctx = jax.nn.silu(c_ctx)
    for i in range(DEPTH):
        last = i == DEPTH - 1
        m_lat = (s_lat @ ada_w[i] + ada_b[i]).reshape(bsz, 6, D_MODEL)[:, :, None, :]
        m_ctx = (s_ctx @ ada_w[i] + ada_b[i]).reshape(6, D_MODEL)
        hl = modulate(rmsnorm(xl, norm_attn_g[i]), m_lat[:, 0], m_lat[:, 1])
        hc = modulate(rmsnorm(xc, norm_attn_g[i]), m_ctx[0], m_ctx[1])
        j = i // N_MIXERS
        kind = i % N_MIXERS
        if kind == 0:
            ol, oc = neighbourhood_attention(hc, hl, na_w_qkv[j], na_rpb[j], na_w_o[j], not last)
        elif kind == 1:
            ol, oc = sliding_window_gqa(hc, hl, sw_w_qkv[j], sw_sinks[j], sw_w_o[j], not last)
        else:
            ol, oc = differential_attention(hc, hl, diff_w_qkv[j], diff_lam_q1[j], diff_lam_k1[j],
                                            diff_lam_q2[j], diff_lam_k2[j], diff_sub_g[j], diff_w_o[j],
                                            0.8 - 0.6 * math.exp(-0.3 * i), not last)
        xl = xl + m_lat[:, 2] * ol
        hl = modulate(rmsnorm(xl, norm_ffn_g[i]), m_lat[:, 3], m_lat[:, 4])
        if last:
            y = moe(hl, router_w[i], router_b[i], exp_w_gu[i], exp_b_gu[i], exp_w_down[i], exp_b_down[i])
            xl = xl + m_lat[:, 5] * y
        else:
            xc = xc + m_ctx[2] * oc
            hc = modulate(rmsnorm(xc, norm_ffn_g[i]), m_ctx[3], m_ctx[4])
            y = moe(jnp.concatenate([hc, hl], axis=1), router_w[i], router_b[i], exp_w_gu[i],
                    exp_b_gu[i], exp_w_down[i], exp_b_down[i])
            xc = xc + m_ctx[5] * y[:, :n_ctx]
            xl = xl + m_lat[:, 5] * y[:, n_ctx:]
    return rmsnorm(xl, final_g)
```

```python
import functools
import math

import jax
import jax.numpy as jnp
from jax import lax
from jax.experimental import pallas as pl
from jax.experimental.pallas import tpu as pltpu

D_MODEL = 2048
DEPTH = 4
CTX_LEN = 256
GRID_W = 64
N_MIXERS = 3
HEAD_DIM = 64
N_HEADS = D_MODEL // HEAD_DIM
N_KV_HEADS = N_HEADS // 8
GQA_GROUP = N_HEADS // N_KV_HEADS
N_DIFF_HEADS = N_HEADS // 2
NA_ROWS = 8
NA_COLS = 16
SW_WINDOW = 128
Q_BLOCK = 128
ROPE_BASE = 10000.0
N_EXPERTS = 32
TOP_K = 4
D_EXPERT = D_MODEL // 4
SWIGLU_LIMIT = 7.0
SWIGLU_ALPHA = 1.702
NORM_EPS = 1e-6
NEG_INF = -1e30

LANES = 128
VMEM_LIMIT_BYTES = 52 * 1024 * 1024
ROW_TILE = 256
MM_ROW_TILES = (1088, 1024, 640, 512, 256)
EXPERT_TILE = 256
ADA_ROWS = 16

F32 = jnp.float32
BF16 = jnp.bfloat16


def _params(semantics):
    return pltpu.CompilerParams(dimension_semantics=semantics, vmem_limit_bytes=VMEM_LIMIT_BYTES)


def _mm_row_tile(nt):
    return next(t for t in MM_ROW_TILES if nt % t == 0)


def _dot_nt(a, b):
    return lax.dot_general(a, b, (((1,), (1,)), ((), ())), preferred_element_type=F32)


def _ada_kernel(s_ref, w_ref, b_ref, o_ref):
    o_ref[0] = jnp.dot(s_ref[...], w_ref[0].astype(BF16), preferred_element_type=F32) + b_ref[0]


def _ada_all(s, ada_w, ada_b):
    depth, d, n = ada_w.shape
    tn = 1024
    return pl.pallas_call(
        _ada_kernel,
        grid=(depth, n // tn),
        in_specs=[
            pl.BlockSpec((ADA_ROWS, d), lambda i, j: (0, 0)),
            pl.BlockSpec((1, d, tn), lambda i, j: (i, 0, j)),
            pl.BlockSpec((1, 1, tn), lambda i, j: (i, 0, j)),
        ],
        out_specs=pl.BlockSpec((1, ADA_ROWS, tn), lambda i, j: (i, 0, j)),
        out_shape=jax.ShapeDtypeStruct((depth, ADA_ROWS, n), F32),
        compiler_params=_params(("arbitrary", "arbitrary")),
        name="ada_ln",
    )(s, ada_w, ada_b.reshape(depth, 1, n))


def _normed(x, g, eps):
    ms = jnp.mean(x * x, axis=-1, keepdims=True)
    return x * lax.rsqrt(ms + eps) * g


def _rms_mod_kernel(x_ref, g_ref, m_ref, o_ref, *, i_shift, i_scale):
    m = m_ref[0, 0]
    y = _normed(x_ref[0], g_ref[...], NORM_EPS)
    o_ref[0] = (y * (1.0 + m[i_scale:i_scale + 1]) + m[i_shift:i_shift + 1]).astype(o_ref.dtype)


def _mods_spec():
    return pl.BlockSpec((1, 1, 6, D_MODEL), lambda b, j: (b, jnp.minimum(j, 1), 0, 0))


def _rms_mod(x, g, mods, i_shift, i_scale):
    bsz, nt, d = x.shape
    return pl.pallas_call(
        functools.partial(_rms_mod_kernel, i_shift=i_shift, i_scale=i_scale),
        grid=(bsz, nt // ROW_TILE),
        in_specs=[
            pl.BlockSpec((1, ROW_TILE, d), lambda b, j: (b, j, 0)),
            pl.BlockSpec((1, d), lambda b, j: (0, 0)),
            _mods_spec(),
        ],
        out_specs=pl.BlockSpec((1, ROW_TILE, d), lambda b, j: (b, j, 0)),
        out_shape=jax.ShapeDtypeStruct((bsz, nt, d), BF16),
        compiler_params=_params(("arbitrary", "arbitrary")),
        name="rms_mod",
    )(x, g.reshape(1, d), mods)


def _rms_router_kernel(x_ref, g_ref, m_ref, wr_ref, br_ref, h_ref, idx_ref, gate_ref, *, i_shift, i_scale):
    m = m_ref[0, 0]
    y = _normed(x_ref[0], g_ref[...], NORM_EPS)
    h = y * (1.0 + m[i_scale:i_scale + 1]) + m[i_shift:i_shift + 1]
    h_ref[0] = h
    logits = jnp.dot(h, wr_ref[...], precision=lax.Precision.HIGHEST, preferred_element_type=F32) + br_ref[...]
    lane = lax.broadcasted_iota(jnp.int32, logits.shape, 1)
    vals, ids = [], []
    for _ in range(TOP_K):
        mx = jnp.max(logits, axis=-1, keepdims=True)
        ik = jnp.min(jnp.where(logits == mx, lane, LANES), axis=-1, keepdims=True)
        vals.append(mx)
        ids.append(ik)
        logits = jnp.where(lane == ik, -jnp.inf, logits)
    es = [jnp.exp(v - vals[0]) for v in vals]
    inv = 1.0 / (es[0] + es[1] + es[2] + es[3])
    idx = jnp.zeros(lane.shape, jnp.int32)
    gate = jnp.zeros(lane.shape, F32)
    for k in range(TOP_K):
        idx = jnp.where(lane == k, ids[k], idx)
        gate = jnp.where(lane == k, es[k] * inv, gate)
    idx_ref[0] = idx
    gate_ref[0] = gate


def _rms_router(x, g, mods, w_router, b_router, i_shift, i_scale):
    bsz, nt, d = x.shape
    wr = jnp.zeros((d, LANES), F32).at[:, :N_EXPERTS].set(w_router)
    br = jnp.full((1, LANES), NEG_INF, F32).at[0, :N_EXPERTS].set(b_router)
    row = pl.BlockSpec((1, ROW_TILE, d), lambda b, j: (b, j, 0))
    lanes = pl.BlockSpec((1, ROW_TILE, LANES), lambda b, j: (b, j, 0))
    return pl.pallas_call(
        functools.partial(_rms_router_kernel, i_shift=i_shift, i_scale=i_scale),
        grid=(bsz, nt // ROW_TILE),
        in_specs=[
            row,
            pl.BlockSpec((1, d), lambda b, j: (0, 0)),
            _mods_spec(),
            pl.BlockSpec((d, LANES), lambda b, j: (0, 0)),
            pl.BlockSpec((1, LANES), lambda b, j: (0, 0)),
        ],
        out_specs=[row, lanes, lanes],
        out_shape=[
            jax.ShapeDtypeStruct((bsz, nt, d), F32),
            jax.ShapeDtypeStruct((bsz, nt, LANES), jnp.int32),
            jax.ShapeDtypeStruct((bsz, nt, LANES), F32),
        ],
        compiler_params=_params(("arbitrary", "arbitrary")),
        name="rms_router",
    )(x, g.reshape(1, d), mods, wr, br)


def _final_norm_kernel(x_ref, g_ref, o_ref):
    o_ref[0] = _normed(x_ref[0], g_ref[...], NORM_EPS)


def _final_norm(x, g, n_ctx):
    bsz, nt, d = x.shape
    skip = n_ctx // ROW_TILE
    return pl.pallas_call(
        _final_norm_kernel,
        grid=(bsz, (nt - n_ctx) // ROW_TILE),
        in_specs=[
            pl.BlockSpec((1, ROW_TILE, d), lambda b, j: (b, j + skip, 0)),
            pl.BlockSpec((1, d), lambda b, j: (0, 0)),
        ],
        out_specs=pl.BlockSpec((1, ROW_TILE, d), lambda b, j: (b, j, 0)),
        out_shape=jax.ShapeDtypeStruct((bsz, nt - n_ctx, d), F32),
        compiler_params=_params(("arbitrary", "arbitrary")),
        name="final_norm",
    )(x, g.reshape(1, d))


def _rope_tables(n_ctx, n_lat):
    half = HEAD_DIM // 2
    nf = half // 2
    t = jnp.arange(n_lat)
    inv_freq = ROPE_BASE ** (-jnp.arange(nf, dtype=F32) / nf)
    lane = jnp.arange(LANES)
    d = lane % HEAD_DIM
    pos = jnp.where((d < half)[None, :], (t // GRID_W)[:, None], (t % GRID_W)[:, None]).astype(F32)
    ang = pos * inv_freq[d % nf][None, :]
    sign = jnp.where((d % half) < nf, -1.0, 1.0)[None, :]
    cos = jnp.concatenate([jnp.ones((n_ctx, LANES), F32), jnp.cos(ang)], axis=0)
    sin = jnp.concatenate([jnp.zeros((n_ctx, LANES), F32), jnp.sin(ang) * sign], axis=0)
    return cos, sin


def _qkv_kernel(x_ref, w_ref, *rest, n_q_chunks, n_rope_chunks):
    if n_rope_chunks:
        cos_ref, sin_ref, o_ref = rest
    else:
        (o_ref,) = rest
    acc = jnp.dot(x_ref[0], w_ref[...], preferred_element_type=F32)
    chunks = acc.shape[1] // LANES
    j = pl.program_id(2)
    nf = HEAD_DIM // 4
    for c in range(chunks):
        gc = j * chunks + c
        a = acc[:, c * LANES:(c + 1) * LANES]
        if n_rope_chunks:
            lane = lax.broadcasted_iota(jnp.int32, a.shape, 1)
            swapped = jnp.where((lane % (2 * nf)) < nf,
                                pltpu.roll(a, LANES - nf, axis=1), pltpu.roll(a, nf, axis=1))
            use = gc < n_rope_chunks
            a = a * jnp.where(use, cos_ref[...], 1.0) + swapped * jnp.where(use, sin_ref[...], 0.0)
        a = a * jnp.where(gc < n_q_chunks, HEAD_DIM ** -0.5, 1.0)
        o_ref[0, :, c * LANES:(c + 1) * LANES] = a.astype(o_ref.dtype)


def _qkv_proj(h, w, n_q_cols, n_rope_cols, rope):
    bsz, nt, d = h.shape
    n = w.shape[1]
    tn = 512
    tm = _mm_row_tile(nt)
    in_specs = [
        pl.BlockSpec((1, tm, d), lambda b, i, j: (b, i, 0)),
        pl.BlockSpec((d, tn), lambda b, i, j: (0, j)),
    ]
    args = [h, w.astype(BF16)]
    if n_rope_cols:
        in_specs += [pl.BlockSpec((tm, LANES), lambda b, i, j: (i, 0))] * 2
        args += list(rope)
    return pl.pallas_call(
        functools.partial(_qkv_kernel, n_q_chunks=n_q_cols // LANES, n_rope_chunks=n_rope_cols // LANES),
        grid=(bsz, nt // tm, n // tn),
        in_specs=in_specs,
        out_specs=pl.BlockSpec((1, tm, tn), lambda b, i, j: (b, i, j)),
        out_shape=jax.ShapeDtypeStruct((bsz, nt, n), BF16),
        compiler_params=_params(("arbitrary", "arbitrary", "arbitrary")),
        name="qkv_proj",
    )(*args)


def _out_proj_kernel(o_ref, w_ref, x_ref, m_ref, y_ref, *, n_ctx, i_gate):
    acc = jnp.dot(o_ref[0], w_ref[...], preferred_element_type=F32)
    tm = acc.shape[0]
    row = pl.program_id(1) * tm + lax.broadcasted_iota(jnp.int32, (tm, 1), 0)
    m = m_ref[0]
    gate = jnp.where(row < n_ctx, m[0, i_gate:i_gate + 1], m[1, i_gate:i_gate + 1])
    y_ref[0] = x_ref[0] + gate * acc


def _out_proj(o, w, x, mods, n_ctx, i_gate):
    bsz, nt, d = o.shape
    n = w.shape[1]
    tn = 512
    tm = _mm_row_tile(nt)
    return pl.pallas_call(
        functools.partial(_out_proj_kernel, n_ctx=n_ctx, i_gate=i_gate),
        grid=(bsz, nt // tm, n // tn),
        in_specs=[
            pl.BlockSpec((1, tm, d), lambda b, i, j: (b, i, 0)),
            pl.BlockSpec((d, tn), lambda b, i, j: (0, j)),
            pl.BlockSpec((1, tm, tn), lambda b, i, j: (b, i, j)),
            pl.BlockSpec((1, 2, 6, tn), lambda b, i, j: (b, 0, 0, j)),
        ],
        out_specs=pl.BlockSpec((1, tm, tn), lambda b, i, j: (b, i, j)),
        out_shape=jax.ShapeDtypeStruct((bsz, nt, n), F32),
        compiler_params=_params(("arbitrary", "arbitrary", "arbitrary")),
        name="out_proj",
    )(o, w.astype(BF16), x, mods)


def _softmax_pv(s, v, extra=None):
    mx = jnp.max(s, axis=-1, keepdims=True)
    if extra is not None:
        mx = jnp.maximum(mx, extra)
    e = jnp.exp(s - mx)
    den = jnp.sum(e, axis=-1, keepdims=True)
    if extra is not None:
        den = den + jnp.exp(extra - mx)
    return jnp.dot(e.astype(BF16), v, preferred_element_type=F32) / den


def _na_bias_table(rpb):
    qc = jnp.arange(GRID_W)[:, None]
    kc = jnp.arange(GRID_W)[None, :]
    win = jnp.clip(qc - NA_COLS // 2, 0, GRID_W - NA_COLS)
    ok = (kc >= win) & (kc < win + NA_COLS)
    col_off = jnp.clip(kc - qc + NA_COLS - 1, 0, 2 * NA_COLS - 2)
    t = jnp.arange(NA_ROWS)[:, None]
    i = jnp.arange(NA_ROWS)[None, :]
    row_off = i - t + NA_ROWS - 1
    tab = rpb[:, row_off][:, :, :, col_off]
    tab = jnp.where(ok[None, None, None], tab, NEG_INF)
    tab = tab.transpose(0, 1, 3, 2, 4)
    return tab.reshape(rpb.shape[0], NA_ROWS, GRID_W, NA_ROWS * GRID_W).astype(F32)


def _na_kernel(q_ref, k_ref, v_ref, bias_ref, o_ref, *, n_ctx, rows):
    hd = HEAD_DIM
    n_loc = NA_ROWS * GRID_W
    outs = []
    for hh in range(2):
        sl = slice(hh * hd, (hh + 1) * hd)
        s = _dot_nt(q_ref[0, 0:n_ctx, sl], k_ref[0, 0:n_ctx, sl])
        outs.append(_softmax_pv(s, v_ref[0, 0:n_ctx, sl]))
    o_ref[0, 0:n_ctx, :] = jnp.concatenate(outs, axis=1).astype(o_ref.dtype)

    def row_body(r, carry):
        rs = jnp.clip(r - NA_ROWS // 2, 0, rows - NA_ROWS)
        t = r - rs
        q0 = pl.multiple_of(n_ctx + r * GRID_W, GRID_W)
        k0 = pl.multiple_of(n_ctx + rs * GRID_W, GRID_W)
        outs = []
        for hh in range(2):
            sl = slice(hh * hd, (hh + 1) * hd)
            q = q_ref[0, pl.ds(q0, GRID_W), sl]
            kk = jnp.concatenate([k_ref[0, pl.ds(k0, n_loc), sl], k_ref[0, 0:n_ctx, sl]], axis=0)
            vv = jnp.concatenate([v_ref[0, pl.ds(k0, n_loc), sl], v_ref[0, 0:n_ctx, sl]], axis=0)
            s = _dot_nt(q, kk)
            s = jnp.concatenate([s[:, :n_loc] + bias_ref[hh, t], s[:, n_loc:]], axis=1)
            outs.append(_softmax_pv(s, vv))
        o_ref[0, pl.ds(q0, GRID_W), :] = jnp.concatenate(outs, axis=1).astype(o_ref.dtype)
        return carry

    lax.fori_loop(0, rows, row_body, 0)


def _na_attention(qkv, bias_tab, n_ctx):
    bsz, nt, _ = qkv.shape
    rows = (nt - n_ctx) // GRID_W
    n_hp = N_HEADS // 2
    blk = lambda off: pl.BlockSpec((1, nt, LANES), lambda p, b: (b, 0, p + off))
    return pl.pallas_call(
        functools.partial(_na_kernel, n_ctx=n_ctx, rows=rows),
        grid=(n_hp, bsz),
        in_specs=[
            blk(0), blk(n_hp), blk(2 * n_hp),
            pl.BlockSpec((2, NA_ROWS, GRID_W, NA_ROWS * GRID_W), lambda p, b: (p, 0, 0, 0)),
        ],
        out_specs=pl.BlockSpec((1, nt, LANES), lambda p, b: (b, 0, p)),
        out_shape=jax.ShapeDtypeStruct((bsz, nt, D_MODEL), BF16),
        compiler_params=_params(("arbitrary", "arbitrary")),
        name="na_attention",
    )(qkv, qkv, qkv, bias_tab)


def _sw_kernel(sink_ref, q_ref, k_ref, v_ref, o_ref, *, n_ctx, n_lat):
    hd = HEAD_DIM
    qb = Q_BLOCK
    n_blk = n_lat // qb
    p = pl.program_id(1)
    jl = pl.program_id(2) - n_ctx // qb
    starts, valids = [], []
    for c in range(3):
        blk = jl - 1 + c
        starts.append(pl.multiple_of(n_ctx + qb * jnp.clip(blk, 0, n_blk - 1), qb))
        valids.append((blk >= 0) & (blk < n_blk) & (jl >= 0))
    iq = lax.broadcasted_iota(jnp.int32, (qb, qb), 0)
    ik = lax.broadcasted_iota(jnp.int32, (qb, qb), 1)
    oks = [valids[c] & (jnp.abs(iq - ik - (c - 1) * qb) <= SW_WINDOW) for c in range(3)]
    bias = jnp.concatenate([jnp.where(ok, 0.0, NEG_INF) for ok in oks]
                           + [jnp.zeros((qb, n_ctx), F32)], axis=1)
    for g in range(2):
        sl = slice(g * hd, (g + 1) * hd)
        kk = jnp.concatenate([k_ref[0, pl.ds(s0, qb), sl] for s0 in starts] + [k_ref[0, 0:n_ctx, sl]], axis=0)
        vv = jnp.concatenate([v_ref[0, pl.ds(s0, qb), sl] for s0 in starts] + [v_ref[0, 0:n_ctx, sl]], axis=0)
        for hq in range(GQA_GROUP):
            h_loc = g * GQA_GROUP + hq
            sink = sink_ref[p * 2 * GQA_GROUP + h_loc]
            s = _dot_nt(q_ref[0, :, h_loc * hd:(h_loc + 1) * hd], kk)
            s = jnp.where(bias < 0.0, NEG_INF, s)
            o = _softmax_pv(s, vv, extra=sink)
            o_ref[0, :, h_loc * hd:(h_loc + 1) * hd] = o.astype(o_ref.dtype)


def _sw_attention(qkv, sinks, n_ctx):
    bsz, nt, _ = qkv.shape
    n_lat = nt - n_ctx
    qw = 2 * GQA_GROUP * HEAD_DIM
    k_blk = D_MODEL // LANES
    v_blk = k_blk + N_KV_HEADS * HEAD_DIM // LANES
    grid_spec = pltpu.PrefetchScalarGridSpec(
        num_scalar_prefetch=1,
        grid=(bsz, N_KV_HEADS // 2, nt // Q_BLOCK),
        in_specs=[
            pl.BlockSpec((1, Q_BLOCK, qw), lambda b, p, j, s: (b, j, p)),
            pl.BlockSpec((1, nt, LANES), lambda b, p, j, s: (b, 0, k_blk + p)),
            pl.BlockSpec((1, nt, LANES), lambda b, p, j, s: (b, 0, v_blk + p)),
        ],
        out_specs=pl.BlockSpec((1, Q_BLOCK, qw), lambda b, p, j, s: (b, j, p)),
    )
    return pl.pallas_call(
        functools.partial(_sw_kernel, n_ctx=n_ctx, n_lat=n_lat),
        grid_spec=grid_spec,
        out_shape=jax.ShapeDtypeStruct((bsz, nt, D_MODEL), BF16),
        compiler_params=_params(("arbitrary", "arbitrary", "arbitrary")),
        name="sw_attention",
    )(sinks.astype(F32), qkv, qkv, qkv)


def _diff_kernel(lam_ref, q_ref, k_ref, v_ref, g_ref, o_ref, *, n_ctx, lambda_init):
    hd = HEAD_DIM
    lam = lam_ref[0]

    def attend(n_keys):
        v = v_ref[0, 0:n_keys, :]
        outs = []
        for i in range(2):
            sl = slice(i * hd, (i + 1) * hd)
            outs.append(_softmax_pv(_dot_nt(q_ref[0, :, sl], k_ref[0, 0:n_keys, sl]), v))
        o = outs[0] - lam * outs[1]
        o = _normed(o, g_ref[...], 1e-5) * (1.0 - lambda_init)
        o_ref[0] = o.astype(o_ref.dtype)

    is_ctx = pl.program_id(2) * q_ref.shape[1] < n_ctx

    @pl.when(is_ctx)
    def _():
        attend(n_ctx)

    @pl.when(jnp.logical_not(is_ctx))
    def _():
        attend(k_ref.shape[1])


def _diff_attention(qkv, lam, sub_g, lambda_init, n_ctx):
    bsz, nt, _ = qkv.shape
    tq = ROW_TILE
    nh = N_DIFF_HEADS
    grid_spec = pltpu.PrefetchScalarGridSpec(
        num_scalar_prefetch=1,
        grid=(bsz, nh, nt // tq),
        in_specs=[
            pl.BlockSpec((1, tq, LANES), lambda b, h, j, s: (b, j, h)),
            pl.BlockSpec((1, nt, LANES), lambda b, h, j, s: (b, 0, nh + h)),
            pl.BlockSpec((1, nt, LANES), lambda b, h, j, s: (b, 0, 2 * nh + h)),
            pl.BlockSpec((1, LANES), lambda b, h, j, s: (0, 0)),
        ],
        out_specs=pl.BlockSpec((1, tq, LANES), lambda b, h, j, s: (b, j, h)),
    )
    return pl.pallas_call(
        functools.partial(_diff_kernel, n_ctx=n_ctx, lambda_init=lambda_init),
        grid_spec=grid_spec,
        out_shape=jax.ShapeDtypeStruct((bsz, nt, D_MODEL), BF16),
        compiler_params=_params(("arbitrary", "arbitrary", "arbitrary")),
        name="diff_attention",
    )(lam.reshape(1).astype(F32), qkv, qkv, qkv, sub_g.reshape(1, LANES).astype(F32))


def _route(idx4):
    n_tok = idx4.shape[0]
    n_pairs = n_tok * TOP_K
    tm = EXPERT_TILE
    n_rows = n_pairs + N_EXPERTS * tm
    n_tiles = n_rows // tm
    flat = idx4.reshape(n_pairs)
    onehot = (flat[:, None] == jnp.arange(N_EXPERTS)[None, :]).astype(jnp.int32)
    csum = jnp.cumsum(onehot, axis=0)
    rank = jnp.take_along_axis(csum, flat[:, None], axis=1)[:, 0] - 1
    counts = csum[-1]
    padded = (counts + tm - 1) // tm * tm
    gend = jnp.cumsum(padded)
    gstart = gend - padded
    pos = gstart[flat] + rank
    tile_start = jnp.arange(n_tiles, dtype=jnp.int32) * tm
    tile_expert = jnp.minimum(jnp.searchsorted(gend, tile_start, side="right"), N_EXPERTS - 1).astype(jnp.int32)
    tile_rows = jnp.clip(counts[tile_expert] - (tile_start - gstart[tile_expert]), 0, tm).astype(jnp.int32)
    src = jnp.zeros((n_rows,), jnp.int32).at[pos].set(jnp.arange(n_pairs, dtype=jnp.int32) // TOP_K)
    return pos.astype(jnp.int32), src.reshape(n_tiles, 1, tm), tile_expert, tile_rows


def _gather_copy(h_hbm, buf, sem, slot):
    return pltpu.make_async_copy(h_hbm.at[pl.ds(0, buf.shape[1])], buf.at[slot], sem.at[slot])


def _ffn_kernel(te_ref, tr_ref, src0_ref, src1_ref, h_hbm, wgu_ref, bgu_ref, wd_ref, bd_ref, y_ref, buf, sem):
    i = pl.program_id(0)
    n = pl.num_programs(0)
    tm = buf.shape[1]
    slot = i % 2

    def issue(src_ref, dst_slot):
        def body(r, carry):
            pltpu.make_async_copy(h_hbm.at[pl.ds(src_ref[0, 0, r], 1)],
                                  buf.at[dst_slot, pl.ds(r, 1)], sem.at[dst_slot]).start()
            return carry
        lax.fori_loop(0, tm, body, 0, unroll=8)

    @pl.when((i == 0) & (tr_ref[0] > 0))
    def _():
        issue(src0_ref, 0)

    nxt = jnp.minimum(i + 1, n - 1)

    @pl.when((i + 1 < n) & (tr_ref[nxt] > 0))
    def _():
        issue(src1_ref, 1 - slot)

    @pl.when(tr_ref[i] > 0)
    def _():
        _gather_copy(h_hbm, buf, sem, slot).wait()
        x = buf[slot].astype(BF16)
        gu = jnp.dot(x, wgu_ref[0], preferred_element_type=F32) + bgu_ref[0]
        gate = jnp.minimum(gu[:, :D_EXPERT], SWIGLU_LIMIT)
        up = jnp.clip(gu[:, D_EXPERT:], -SWIGLU_LIMIT, SWIGLU_LIMIT)
        act = (up + 1.0) * (gate * jax.nn.sigmoid(SWIGLU_ALPHA * gate))
        y_ref[...] = jnp.dot(act.astype(BF16), wd_ref[0], preferred_element_type=F32) + bd_ref[0]

    @pl.when(tr_ref[i] == 0)
    def _():
        y_ref[...] = jnp.zeros(y_ref.shape, y_ref.dtype)


def _expert_ffn(h, src, tile_expert, tile_rows, w_gu, b_gu, w_down, b_down):
    n_tok, d = h.shape
    n_tiles, _, tm = src.shape
    grid_spec = pltpu.PrefetchScalarGridSpec(
        num_scalar_prefetch=2,
        grid=(n_tiles,),
        in_specs=[
            pl.BlockSpec((1, 1, tm), lambda i, te, tr: (i, 0, 0), memory_space=pltpu.SMEM),
            pl.BlockSpec((1, 1, tm), lambda i, te, tr: (jnp.minimum(i + 1, n_tiles - 1), 0, 0),
                         memory_space=pltpu.SMEM),
            pl.BlockSpec(memory_space=pl.ANY),
            pl.BlockSpec((1, d, 2 * D_EXPERT), lambda i, te, tr: (te[i], 0, 0)),
            pl.BlockSpec((1, 1, 2 * D_EXPERT), lambda i, te, tr: (te[i], 0, 0)),
            pl.BlockSpec((1, D_EXPERT, d), lambda i, te, tr: (te[i], 0, 0)),
            pl.BlockSpec((1, 1, d), lambda i, te, tr: (te[i], 0, 0)),
        ],
        out_specs=pl.BlockSpec((tm, d), lambda i, te, tr: (i, 0)),
        scratch_shapes=[pltpu.VMEM((2, tm, d), F32), pltpu.SemaphoreType.DMA((2,))],
    )
    return pl.pallas_call(
        _ffn_kernel,
        grid_spec=grid_spec,
        out_shape=jax.ShapeDtypeStruct((n_tiles * tm, d), F32),
        compiler_params=_params(("arbitrary",)),
        name="expert_ffn",
    )(tile_expert, tile_rows, src, src, h, w_gu.astype(BF16), b_gu.reshape(N_EXPERTS, 1, -1),
      w_down.astype(BF16), b_down.reshape(N_EXPERTS, 1, -1))


def _combine_kernel(pos0_ref, pos1_ref, y_hbm, gate_ref, x_ref, m_ref, o_ref, buf, sem, *, i_gate):
    i = pl.program_id(0)
    n = pl.num_programs(0)
    tc = buf.shape[2]
    slot = i % 2

    def issue(pos_ref, dst_slot):
        def body(r, carry):
            for k in range(TOP_K):
                pltpu.make_async_copy(y_hbm.at[pl.ds(pos_ref[0, 0, r * TOP_K + k], 1)],
                                      buf.at[dst_slot, k, pl.ds(r, 1)], sem.at[dst_slot]).start()
            return carry
        lax.fori_loop(0, tc, body, 0, unroll=2)

    @pl.when(i == 0)
    def _():
        issue(pos0_ref, 0)

    @pl.when(i + 1 < n)
    def _():
        issue(pos1_ref, 1 - slot)

    for k in range(TOP_K):
        pltpu.make_async_copy(y_hbm.at[pl.ds(0, tc)], buf.at[slot, k], sem.at[slot]).wait()
    g = gate_ref[...]
    y = g[:, 0:1] * buf[slot, 0]
    for k in range(1, TOP_K):
        y = y + g[:, k:k + 1] * buf[slot, k]
    o_ref[...] = x_ref[...] + m_ref[0, 0, i_gate:i_gate + 1] * y


def _combine(y, pos, gate, x, mods, i_gate, tiles_per_sample):
    n_tok, d = x.shape
    tc = ROW_TILE
    n_steps = n_tok // tc
    pos3 = pos.reshape(n_steps, 1, tc * TOP_K)

    def mod_map(i):
        return (i // tiles_per_sample, jnp.minimum(i % tiles_per_sample, 1), 0, 0)

    return pl.pallas_call(
        functools.partial(_combine_kernel, i_gate=i_gate),
        grid=(n_steps,),
        in_specs=[
            pl.BlockSpec((1, 1, tc * TOP_K), lambda i: (i, 0, 0), memory_space=pltpu.SMEM),
            pl.BlockSpec((1, 1, tc * TOP_K), lambda i: (jnp.minimum(i + 1, n_steps - 1), 0, 0),
                         memory_space=pltpu.SMEM),
            pl.BlockSpec(memory_space=pl.ANY),
            pl.BlockSpec((tc, LANES), lambda i: (i, 0)),
            pl.BlockSpec((tc, d), lambda i: (i, 0)),
            pl.BlockSpec((1, 1, 6, d), mod_map),
        ],
        out_specs=pl.BlockSpec((tc, d), lambda i: (i, 0)),
        out_shape=jax.ShapeDtypeStruct((n_tok, d), F32),
        scratch_shapes=[pltpu.VMEM((2, TOP_K, tc, d), F32), pltpu.SemaphoreType.DMA((2,))],
        compiler_params=_params(("arbitrary",)),
        name="moe_combine",
    )(pos3, pos3, y, gate, x, mods)


def _moe_block(x, g, mods, w_router, b_router, w_gu, b_gu, w_down, b_down):
    bsz, nt, d = x.shape
    h, idx, gate = _rms_router(x, g, mods, w_router, b_router, i_shift=3, i_scale=4)
    n_tok = bsz * nt
    pos, src, tile_expert, tile_rows = _route(idx.reshape(n_tok, LANES)[:, :TOP_K])
    y = _expert_ffn(h.reshape(n_tok, d), src, tile_expert, tile_rows, w_gu, b_gu, w_down, b_down)
    out = _combine(y, pos, gate.reshape(n_tok, LANES), x.reshape(n_tok, d), mods, 5, nt // ROW_TILE)
    return out.reshape(bsz, nt, d)


def kernel(x, c, ctx, c_ctx, ada_w, ada_b, norm_attn_g, norm_ffn_g, na_w_qkv, na_rpb, na_w_o, sw_w_qkv, sw_sinks, sw_w_o, diff_w_qkv, diff_lam_q1, diff_lam_k1, diff_lam_q2, diff_lam_k2, diff_sub_g, diff_w_o, router_w, router_b, exp_w_gu, exp_b_gu, exp_w_down, exp_b_down, final_g):
    bsz, n_lat, d = x.shape
    n_ctx = ctx.shape[1]
    assert d == D_MODEL and n_ctx == CTX_LEN and n_lat % GRID_W == 0 and bsz < ADA_ROWS
    assert n_lat % ROW_TILE == 0

    cond = jnp.concatenate([c, c_ctx[None, :], jnp.zeros((ADA_ROWS - bsz - 1, d), c.dtype)], axis=0)
    ada = _ada_all(jax.nn.silu(cond).astype(BF16), ada_w, ada_b)
    rope = _rope_tables(n_ctx, n_lat)

    xs = jnp.concatenate([ctx, x], axis=1)
    for i in range(DEPTH):
        m_lat = ada[i, :bsz].reshape(bsz, 6, d)
        m_ctx = jnp.broadcast_to(ada[i, bsz].reshape(1, 6, d), (bsz, 6, d))
        mods = jnp.stack([m_ctx, m_lat], axis=1)
        h = _rms_mod(xs, norm_attn_g[i], mods, i_shift=0, i_scale=1)
        j = i // N_MIXERS
        kind = i % N_MIXERS
        if kind == 0:
            qkv = _qkv_proj(h, na_w_qkv[j], d, 0, None)
            o = _na_attention(qkv, _na_bias_table(na_rpb[j]), n_ctx)
            w_o = na_w_o[j]
        elif kind == 1:
            n_kv = N_KV_HEADS * HEAD_DIM
            qkv = _qkv_proj(h, sw_w_qkv[j], d, d + n_kv, rope)
            o = _sw_attention(qkv, sw_sinks[j], n_ctx)
            w_o = sw_w_o[j]
        else:
            lambda_init = 0.8 - 0.6 * math.exp(-0.3 * i)
            lam = (jnp.exp(jnp.sum(diff_lam_q1[j].astype(F32) * diff_lam_k1[j].astype(F32)))
                   - jnp.exp(jnp.sum(diff_lam_q2[j].astype(F32) * diff_lam_k2[j].astype(F32))) + lambda_init)
            qkv = _qkv_proj(h, diff_w_qkv[j], d, 2 * d, rope)
            o = _diff_attention(qkv, lam, diff_sub_g[j], lambda_init, n_ctx)
            w_o = diff_w_o[j]
        xs = _out_proj(o, w_o, xs, mods, n_ctx, i_gate=2)
        xs = _moe_block(xs, norm_ffn_g[i], mods, router_w[i], router_b[i], exp_w_gu[i], exp_b_gu[i],
                        exp_w_down[i], exp_b_down[i])
    return _final_norm(xs, final_g, n_ctx)
```

```python
import functools
import math

import jax
import jax.numpy as jnp
from jax import lax
from jax.experimental import pallas as pl
from jax.experimental.pallas import tpu as pltpu

D_MODEL = 2048
DEPTH = 4
CTX_LEN = 256
GRID_W = 64
N_MIXERS = 3
HEAD_DIM = 64
N_HEADS = D_MODEL // HEAD_DIM
N_KV_HEADS = N_HEADS // 8
GQA_GROUP = N_HEADS // N_KV_HEADS
N_DIFF_HEADS = N_HEADS // 2
NA_ROWS = 8
NA_COLS = 16
NA_QROWS = 4
NA_WIN = 12
SW_WINDOW = 128
Q_BLOCK = 128
SW_STACK = 4
ROPE_BASE = 10000.0
N_EXPERTS = 32
TOP_K = 4
D_EXPERT = D_MODEL // 4
SWIGLU_LIMIT = 7.0
SWIGLU_ALPHA = 1.702
NORM_EPS = 1e-6
NEG_INF = -1e30
LOG2_E = 1.4426950408889634

LANES = 128
VMEM_LIMIT_BYTES = 52 * 1024 * 1024
ROW_TILE = 256
MM_ROW_TILES = (1088, 1024, 640, 512, 256)
EXPERT_TILE = 256
ADA_ROWS = 16

F32 = jnp.float32
BF16 = jnp.bfloat16


def _params(semantics):
    return pltpu.CompilerParams(dimension_semantics=semantics, vmem_limit_bytes=VMEM_LIMIT_BYTES)


def _mm_row_tile(nt):
    return next(t for t in MM_ROW_TILES if nt % t == 0)


def _dot_nt(a, b):
    return lax.dot_general(a, b, (((1,), (1,)), ((), ())), preferred_element_type=F32)


def _ada_kernel(s_ref, w_ref, b_ref, o_ref):
    o_ref[0] = jnp.dot(s_ref[...], w_ref[0].astype(BF16), preferred_element_type=F32) + b_ref[0]


def _ada_all(s, ada_w, ada_b):
    depth, d, n = ada_w.shape
    tn = 1024
    return pl.pallas_call(
        _ada_kernel,
        grid=(depth, n // tn),
        in_specs=[
            pl.BlockSpec((ADA_ROWS, d), lambda i, j: (0, 0)),
            pl.BlockSpec((1, d, tn), lambda i, j: (i, 0, j)),
            pl.BlockSpec((1, 1, tn), lambda i, j: (i, 0, j)),
        ],
        out_specs=pl.BlockSpec((1, ADA_ROWS, tn), lambda i, j: (i, 0, j)),
        out_shape=jax.ShapeDtypeStruct((depth, ADA_ROWS, n), F32),
        compiler_params=_params(("arbitrary", "arbitrary")),
        name="ada_ln",
    )(s, ada_w, ada_b.reshape(depth, 1, n))


def _normed(x, g, eps):
    ms = jnp.mean(x * x, axis=-1, keepdims=True)
    return x * lax.rsqrt(ms + eps) * g


def _rms_mod_kernel(x_ref, g_ref, m_ref, o_ref, *, i_shift, i_scale):
    m = m_ref[0, 0]
    y = _normed(x_ref[0], g_ref[...], NORM_EPS)
    o_ref[0] = (y * (1.0 + m[i_scale:i_scale + 1]) + m[i_shift:i_shift + 1]).astype(o_ref.dtype)


def _mods_spec():
    return pl.BlockSpec((1, 1, 6, D_MODEL), lambda b, j: (b, jnp.minimum(j, 1), 0, 0))


def _rms_mod(x, g, mods, i_shift, i_scale):
    bsz, nt, d = x.shape
    return pl.pallas_call(
        functools.partial(_rms_mod_kernel, i_shift=i_shift, i_scale=i_scale),
        grid=(bsz, nt // ROW_TILE),
        in_specs=[
            pl.BlockSpec((1, ROW_TILE, d), lambda b, j: (b, j, 0)),
            pl.BlockSpec((1, d), lambda b, j: (0, 0)),
            _mods_spec(),
        ],
        out_specs=pl.BlockSpec((1, ROW_TILE, d), lambda b, j: (b, j, 0)),
        out_shape=jax.ShapeDtypeStruct((bsz, nt, d), BF16),
        compiler_params=_params(("arbitrary", "arbitrary")),
        name="rms_mod",
    )(x, g.reshape(1, d), mods)


def _rms_router_kernel(x_ref, g_ref, m_ref, wr_ref, br_ref, h_ref, idx_ref, gate_ref, *, i_shift, i_scale):
    m = m_ref[0, 0]
    y = _normed(x_ref[0], g_ref[...], NORM_EPS)
    h = y * (1.0 + m[i_scale:i_scale + 1]) + m[i_shift:i_shift + 1]
    h_ref[0] = h
    h_hi = h.astype(BF16)
    h_lo = (h - h_hi.astype(F32)).astype(BF16)
    part = (jnp.dot(h_hi, wr_ref[...], preferred_element_type=F32)
            + jnp.dot(h_lo, wr_ref[...], preferred_element_type=F32))
    logits = part + pltpu.roll(part, LANES - N_EXPERTS, axis=1) + br_ref[...]
    lane = lax.broadcasted_iota(jnp.int32, logits.shape, 1)
    logits = jnp.where(lane < N_EXPERTS, logits, NEG_INF)
    vals, ids = [], []
    for _ in range(TOP_K):
        mx = jnp.max(logits, axis=-1, keepdims=True)
        ik = jnp.min(jnp.where(logits == mx, lane, LANES), axis=-1, keepdims=True)
        vals.append(mx)
        ids.append(ik)
        logits = jnp.where(lane == ik, -jnp.inf, logits)
    es = [jnp.exp(v - vals[0]) for v in vals]
    inv = 1.0 / (es[0] + es[1] + es[2] + es[3])
    idx = jnp.zeros(lane.shape, jnp.int32)
    gate = jnp.zeros(lane.shape, F32)
    for k in range(TOP_K):
        idx = jnp.where(lane == k, ids[k], idx)
        gate = jnp.where(lane == k, es[k] * inv, gate)
    idx_ref[0] = idx
    gate_ref[0] = gate


def _rms_router(x, g, mods, w_router, b_router, i_shift, i_scale):
    bsz, nt, d = x.shape
    w_hi = w_router.astype(BF16)
    w_lo = (w_router - w_hi.astype(F32)).astype(BF16)
    wr = jnp.zeros((d, LANES), BF16).at[:, :N_EXPERTS].set(w_hi).at[:, N_EXPERTS:2 * N_EXPERTS].set(w_lo)
    br = jnp.zeros((1, LANES), F32).at[0, :N_EXPERTS].set(b_router)
    row = pl.BlockSpec((1, ROW_TILE, d), lambda b, j: (b, j, 0))
    lanes = pl.BlockSpec((1, ROW_TILE, LANES), lambda b, j: (b, j, 0))
    return pl.pallas_call(
        functools.partial(_rms_router_kernel, i_shift=i_shift, i_scale=i_scale),
        grid=(bsz, nt // ROW_TILE),
        in_specs=[
            row,
            pl.BlockSpec((1, d), lambda b, j: (0, 0)),
            _mods_spec(),
            pl.BlockSpec((d, LANES), lambda b, j: (0, 0)),
            pl.BlockSpec((1, LANES), lambda b, j: (0, 0)),
        ],
        out_specs=[row, lanes, lanes],
        out_shape=[
            jax.ShapeDtypeStruct((bsz, nt, d), F32),
            jax.ShapeDtypeStruct((bsz, nt, LANES), jnp.int32),
            jax.ShapeDtypeStruct((bsz, nt, LANES), F32),
        ],
        compiler_params=_params(("arbitrary", "arbitrary")),
        name="rms_router",
    )(x, g.reshape(1, d), mods, wr, br)


def _final_norm_kernel(x_ref, g_ref, o_ref):
    o_ref[0] = _normed(x_ref[0], g_ref[...], NORM_EPS)


def _final_norm(x, g, n_ctx):
    bsz, nt, d = x.shape
    skip = n_ctx // ROW_TILE
    return pl.pallas_call(
        _final_norm_kernel,
        grid=(bsz, (nt - n_ctx) // ROW_TILE),
        in_specs=[
            pl.BlockSpec((1, ROW_TILE, d), lambda b, j: (b, j + skip, 0)),
            pl.BlockSpec((1, d), lambda b, j: (0, 0)),
        ],
        out_specs=pl.BlockSpec((1, ROW_TILE, d), lambda b, j: (b, j, 0)),
        out_shape=jax.ShapeDtypeStruct((bsz, nt - n_ctx, d), F32),
        compiler_params=_params(("arbitrary", "arbitrary")),
        name="final_norm",
    )(x, g.reshape(1, d))


def _rope_tables(n_ctx, n_lat):
    half = HEAD_DIM // 2
    nf = half // 2
    t = jnp.arange(n_lat)
    inv_freq = ROPE_BASE ** (-jnp.arange(nf, dtype=F32) / nf)
    lane = jnp.arange(LANES)
    d = lane % HEAD_DIM
    pos = jnp.where((d < half)[None, :], (t // GRID_W)[:, None], (t % GRID_W)[:, None]).astype(F32)
    ang = pos * inv_freq[d % nf][None, :]
    sign = jnp.where((d % half) < nf, -1.0, 1.0)[None, :]
    cos = jnp.concatenate([jnp.ones((n_ctx, LANES), F32), jnp.cos(ang)], axis=0)
    sin = jnp.concatenate([jnp.zeros((n_ctx, LANES), F32), jnp.sin(ang) * sign], axis=0)
    return cos, sin


def _qkv_kernel(x_ref, w_ref, *rest, n_q_chunks, n_rope_chunks, q_scale):
    if n_rope_chunks:
        cos_ref, sin_ref, o_ref = rest
    else:
        (o_ref,) = rest
    acc = jnp.dot(x_ref[0], w_ref[...], preferred_element_type=F32)
    chunks = acc.shape[1] // LANES
    j = pl.program_id(2)
    nf = HEAD_DIM // 4
    for c in range(chunks):
        gc = j * chunks + c
        a = acc[:, c * LANES:(c + 1) * LANES]
        if n_rope_chunks:
            lane = lax.broadcasted_iota(jnp.int32, a.shape, 1)
            swapped = jnp.where((lane % (2 * nf)) < nf,
                                pltpu.roll(a, LANES - nf, axis=1), pltpu.roll(a, nf, axis=1))
            use = gc < n_rope_chunks
            a = a * jnp.where(use, cos_ref[...], 1.0) + swapped * jnp.where(use, sin_ref[...], 0.0)
        a = a * jnp.where(gc < n_q_chunks, q_scale, 1.0)
        o_ref[0, :, c * LANES:(c + 1) * LANES] = a.astype(o_ref.dtype)


def _qkv_proj(h, w, n_q_cols, n_rope_cols, rope, q_scale=HEAD_DIM ** -0.5):
    bsz, nt, d = h.shape
    n = w.shape[1]
    tn = 512
    tm = _mm_row_tile(nt)
    in_specs = [
        pl.BlockSpec((1, tm, d), lambda b, i, j: (b, i, 0)),
        pl.BlockSpec((d, tn), lambda b, i, j: (0, j)),
    ]
    args = [h, w.astype(BF16)]
    if n_rope_cols:
        in_specs += [pl.BlockSpec((tm, LANES), lambda b, i, j: (i, 0))] * 2
        args += list(rope)
    return pl.pallas_call(
        functools.partial(_qkv_kernel, n_q_chunks=n_q_cols // LANES, n_rope_chunks=n_rope_cols // LANES,
                          q_scale=q_scale),
        grid=(bsz, nt // tm, n // tn),
        in_specs=in_specs,
        out_specs=pl.BlockSpec((1, tm, tn), lambda b, i, j: (b, i, j)),
        out_shape=jax.ShapeDtypeStruct((bsz, nt, n), BF16),
        compiler_params=_params(("arbitrary", "arbitrary", "arbitrary")),
        name="qkv_proj",
    )(*args)


def _out_proj_kernel(o_ref, w_ref, x_ref, m_ref, y_ref, *, n_ctx, i_gate):
    acc = jnp.dot(o_ref[0], w_ref[...], preferred_element_type=F32)
    tm = acc.shape[0]
    row = pl.program_id(1) * tm + lax.broadcasted_iota(jnp.int32, (tm, 1), 0)
    m = m_ref[0]
    gate = jnp.where(row < n_ctx, m[0, i_gate:i_gate + 1], m[1, i_gate:i_gate + 1])
    y_ref[0] = x_ref[0] + gate * acc


def _out_proj(o, w, x, mods, n_ctx, i_gate):
    bsz, nt, d = o.shape
    n = w.shape[1]
    tn = 512
    tm = _mm_row_tile(nt)
    return pl.pallas_call(
        functools.partial(_out_proj_kernel, n_ctx=n_ctx, i_gate=i_gate),
        grid=(bsz, nt // tm, n // tn),
        in_specs=[
            pl.BlockSpec((1, tm, d), lambda b, i, j: (b, i, 0)),
            pl.BlockSpec((d, tn), lambda b, i, j: (0, j)),
            pl.BlockSpec((1, tm, tn), lambda b, i, j: (b, i, j)),
            pl.BlockSpec((1, 2, 6, tn), lambda b, i, j: (b, 0, 0, j)),
        ],
        out_specs=pl.BlockSpec((1, tm, tn), lambda b, i, j: (b, i, j)),
        out_shape=jax.ShapeDtypeStruct((bsz, nt, n), F32),
        compiler_params=_params(("arbitrary", "arbitrary", "arbitrary")),
        name="out_proj",
    )(o, w.astype(BF16), x, mods)


def _softmax_pv(s, v, extra=None, base2=False):
    exp = jnp.exp2 if base2 else jnp.exp
    mx = jnp.max(s, axis=-1, keepdims=True)
    if extra is not None:
        mx = jnp.maximum(mx, extra)
    e = exp(s - mx)
    den = jnp.sum(e, axis=-1, keepdims=True)
    if extra is not None:
        den = den + exp(extra - mx)
    return jnp.dot(e.astype(BF16), v, preferred_element_type=F32) / den


def _split_heads(x):
    lo = lax.broadcasted_iota(jnp.int32, x.shape, 1) < HEAD_DIM
    zero = jnp.zeros_like(x)
    return jnp.concatenate([jnp.where(lo, x, zero), jnp.where(lo, zero, x)], axis=0)


def _merge_heads(o):
    n = o.shape[0] // 2
    lo = lax.broadcasted_iota(jnp.int32, (n, o.shape[1]), 1) < HEAD_DIM
    return jnp.where(lo, o[:n], o[n:])


def _na_block_geometry(rows):
    n_blk = rows // NA_QROWS
    r0 = [j * NA_QROWS for j in range(n_blk)]
    ks = [min(max(r - NA_ROWS // 2, 0), rows - NA_WIN) for r in r0]
    cfg = [0 if j == 0 else (2 if j == n_blk - 1 else 1) for j in range(n_blk)]
    return r0, ks, cfg


def _na_bias_table(rpb, rows):
    r0s, kss, cfgs = _na_block_geometry(rows)
    assert all(r0s[j] - kss[j] == r0s[1] - kss[1] for j in range(len(cfgs)) if cfgs[j] == 1)
    rep = {c: j for j, c in reversed(list(enumerate(cfgs)))}
    qc = jnp.arange(GRID_W)[:, None]
    kc = jnp.arange(GRID_W)[None, :]
    win = jnp.clip(qc - NA_COLS // 2, 0, GRID_W - NA_COLS)
    col_ok = (kc >= win) & (kc < win + NA_COLS)
    col_off = jnp.clip(kc - qc + NA_COLS - 1, 0, 2 * NA_COLS - 2)
    tabs = []
    for c in range(3):
        r = r0s[rep[c]] + jnp.arange(NA_QROWS)[:, None]
        kr = kss[rep[c]] + jnp.arange(NA_WIN)[None, :]
        rs = jnp.clip(r - NA_ROWS // 2, 0, rows - NA_ROWS)
        row_ok = (kr >= rs) & (kr < rs + NA_ROWS)
        row_off = jnp.clip(kr - r + NA_ROWS - 1, 0, 2 * NA_ROWS - 2)
        t = rpb[:, row_off][:, :, :, col_off]
        ok = row_ok[:, :, None, None] & col_ok[None, None]
        t = jnp.where(ok[None], t, NEG_INF).transpose(0, 1, 3, 2, 4)
        tabs.append(t.reshape(rpb.shape[0], NA_QROWS * GRID_W, NA_WIN * GRID_W))
    h = rpb.shape[0]
    tab = jnp.stack(tabs, axis=1).reshape(h // 2, 2, 3, NA_QROWS * GRID_W, NA_WIN * GRID_W)
    tab = tab.transpose(0, 2, 1, 3, 4)
    return tab.reshape(h // 2, 3, 2 * NA_QROWS * GRID_W, NA_WIN * GRID_W).astype(F32)


def _na_kernel(q_ref, k_ref, v_ref, bias_ref, o_ref, *, n_ctx, rows):
    nq = NA_QROWS * GRID_W
    nwin = NA_WIN * GRID_W
    n_blk = rows // NA_QROWS
    s = _dot_nt(_split_heads(q_ref[0, 0:n_ctx, :]), k_ref[0, 0:n_ctx, :])
    o_ref[0, 0:n_ctx, :] = _merge_heads(_softmax_pv(s, v_ref[0, 0:n_ctx, :])).astype(o_ref.dtype)

    def block(j, carry):
        r0 = j * NA_QROWS
        ks = jnp.clip(r0 - NA_ROWS // 2, 0, rows - NA_WIN)
        cfg = jnp.where(j == 0, 0, jnp.where(j == n_blk - 1, 2, 1))
        q0 = pl.multiple_of(n_ctx + r0 * GRID_W, GRID_W)
        k0 = pl.multiple_of(n_ctx + ks * GRID_W, GRID_W)
        kk = jnp.concatenate([k_ref[0, 0:n_ctx, :], k_ref[0, pl.ds(k0, nwin), :]], axis=0)
        vv = jnp.concatenate([v_ref[0, 0:n_ctx, :], v_ref[0, pl.ds(k0, nwin), :]], axis=0)
        s = _dot_nt(_split_heads(q_ref[0, pl.ds(q0, nq), :]), kk)
        s = jnp.concatenate([s[:, :n_ctx], s[:, n_ctx:] + bias_ref[0, cfg]], axis=1)
        o_ref[0, pl.ds(q0, nq), :] = _merge_heads(_softmax_pv(s, vv)).astype(o_ref.dtype)
        return carry

    lax.fori_loop(0, n_blk, block, 0, unroll=4)


def _na_attention(qkv, bias_tab, n_ctx):
    bsz, nt, _ = qkv.shape
    rows = (nt - n_ctx) // GRID_W
    n_hp = N_HEADS // 2
    blk = lambda off: pl.BlockSpec((1, nt, LANES), lambda p, b: (b, 0, p + off))
    return pl.pallas_call(
        functools.partial(_na_kernel, n_ctx=n_ctx, rows=rows),
        grid=(n_hp, bsz),
        in_specs=[
            blk(0), blk(n_hp), blk(2 * n_hp),
            pl.BlockSpec((1,) + bias_tab.shape[1:], lambda p, b: (p, 0, 0, 0)),
        ],
        out_specs=pl.BlockSpec((1, nt, LANES), lambda p, b: (b, 0, p)),
        out_shape=jax.ShapeDtypeStruct((bsz, nt, D_MODEL), BF16),
        compiler_params=_params(("arbitrary", "arbitrary")),
        name="na_attention",
    )(qkv, qkv, qkv, bias_tab)


def _sw_kernel(sink_ref, q_ref, k_ref, v_ref, o_ref, *, n_ctx, n_lat):
    hd = HEAD_DIM
    qb = Q_BLOCK
    n_blk = n_lat // qb
    p = pl.program_id(1)
    jl = pl.program_id(2) - n_ctx // qb
    starts, valids = [], []
    for c in range(3):
        blk = jl - 1 + c
        starts.append(pl.multiple_of(n_ctx + qb * jnp.clip(blk, 0, n_blk - 1), qb))
        valids.append((blk >= 0) & (blk < n_blk) & (jl >= 0))
    iq = lax.broadcasted_iota(jnp.int32, (qb, qb), 0)
    ik = lax.broadcasted_iota(jnp.int32, (qb, qb), 1)
    oks = [valids[c] & (jnp.abs(iq - ik - (c - 1) * qb) <= SW_WINDOW) for c in range(3)]
    ok = jnp.concatenate(oks + [jnp.full((qb, n_ctx), True)], axis=1)
    k_all = jnp.concatenate([k_ref[0, pl.ds(s0, qb), :] for s0 in starts] + [k_ref[0, 0:n_ctx, :]], axis=0)
    v_all = jnp.concatenate([v_ref[0, pl.ds(s0, qb), :] for s0 in starts] + [v_ref[0, 0:n_ctx, :]], axis=0)
    gw = GQA_GROUP * hd
    for g in range(2):
        kg = jnp.concatenate([k_all[:, g * hd:(g + 1) * hd]] * 2, axis=1)
        vg = jnp.concatenate([v_all[:, g * hd:(g + 1) * hd]] * 2, axis=1)
        for c0 in range(0, GQA_GROUP // 2, SW_STACK // 2):
            cols = [g * gw + c * LANES for c in range(c0, c0 + SW_STACK // 2)]
            qs = jnp.concatenate([_split_heads(q_ref[0, :, c:c + LANES]) for c in cols], axis=0)
            s = _dot_nt(qs, kg)
            s = jnp.concatenate([jnp.where(ok, s[i * qb:(i + 1) * qb], NEG_INF) for i in range(SW_STACK)], axis=0)
            sink = jnp.concatenate([jnp.full((qb, 1), sink_ref[(p * 2 + g) * GQA_GROUP + 2 * c0 + i], F32)
                                    for i in range(SW_STACK)], axis=0)
            o = _softmax_pv(s, vg, extra=sink)
            for n, c in enumerate(cols):
                o_ref[0, :, c:c + LANES] = _merge_heads(o[2 * n * qb:(2 * n + 2) * qb]).astype(o_ref.dtype)


def _sw_attention(qkv, sinks, n_ctx):
    bsz, nt, _ = qkv.shape
    n_lat = nt - n_ctx
    qw = 2 * GQA_GROUP * HEAD_DIM
    k_blk = D_MODEL // LANES
    v_blk = k_blk + N_KV_HEADS * HEAD_DIM // LANES
    grid_spec = pltpu.PrefetchScalarGridSpec(
        num_scalar_prefetch=1,
        grid=(bsz, N_KV_HEADS // 2, nt // Q_BLOCK),
        in_specs=[
            pl.BlockSpec((1, Q_BLOCK, qw), lambda b, p, j, s: (b, j, p)),
            pl.BlockSpec((1, nt, LANES), lambda b, p, j, s: (b, 0, k_blk + p)),
            pl.BlockSpec((1, nt, LANES), lambda b, p, j, s: (b, 0, v_blk + p)),
        ],
        out_specs=pl.BlockSpec((1, Q_BLOCK, qw), lambda b, p, j, s: (b, j, p)),
    )
    return pl.pallas_call(
        functools.partial(_sw_kernel, n_ctx=n_ctx, n_lat=n_lat),
        grid_spec=grid_spec,
        out_shape=jax.ShapeDtypeStruct((bsz, nt, D_MODEL), BF16),
        compiler_params=_params(("arbitrary", "arbitrary", "arbitrary")),
        name="sw_attention",
    )(sinks.astype(F32), qkv, qkv, qkv)


def _diff_kernel(lam_ref, q_ref, k_ref, v_ref, g_ref, o_ref, *, n_ctx, lambda_init):
    hd = HEAD_DIM
    lam = lam_ref[0]

    def attend(n_keys):
        tq = q_ref.shape[1]
        qs = _split_heads(q_ref[0])
        k = k_ref[0, 0:n_keys, :]
        v = v_ref[0, 0:n_keys, :]
        outs = [_softmax_pv(_dot_nt(qs[i * tq:(i + 1) * tq], k), v, base2=True) for i in range(2)]
        o = outs[0] - lam * outs[1]
        o = _normed(o, g_ref[...], 1e-5) * (1.0 - lambda_init)
        o_ref[0] = o.astype(o_ref.dtype)

    is_ctx = pl.program_id(2) * q_ref.shape[1] < n_ctx

    @pl.when(is_ctx)
    def _():
        attend(n_ctx)

    @pl.when(jnp.logical_not(is_ctx))
    def _():
        attend(k_ref.shape[1])


def _diff_attention(qkv, lam, sub_g, lambda_init, n_ctx):
    bsz, nt, _ = qkv.shape
    tq = ROW_TILE
    nh = N_DIFF_HEADS
    grid_spec = pltpu.PrefetchScalarGridSpec(
        num_scalar_prefetch=1,
        grid=(bsz, nh, nt // tq),
        in_specs=[
            pl.BlockSpec((1, tq, LANES), lambda b, h, j, s: (b, j, h)),
            pl.BlockSpec((1, nt, LANES), lambda b, h, j, s: (b, 0, nh + h)),
            pl.BlockSpec((1, nt, LANES), lambda b, h, j, s: (b, 0, 2 * nh + h)),
            pl.BlockSpec((1, LANES), lambda b, h, j, s: (0, 0)),
        ],
        out_specs=pl.BlockSpec((1, tq, LANES), lambda b, h, j, s: (b, j, h)),
    )
    return pl.pallas_call(
        functools.partial(_diff_kernel, n_ctx=n_ctx, lambda_init=lambda_init),
        grid_spec=grid_spec,
        out_shape=jax.ShapeDtypeStruct((bsz, nt, D_MODEL), BF16),
        compiler_params=_params(("arbitrary", "arbitrary", "arbitrary")),
        name="diff_attention",
    )(lam.reshape(1).astype(F32), qkv, qkv, qkv, sub_g.reshape(1, LANES).astype(F32))


def _route(idx4):
    n_tok = idx4.shape[0]
    n_pairs = n_tok * TOP_K
    tm = EXPERT_TILE
    n_rows = n_pairs + N_EXPERTS * tm
    n_tiles = n_rows // tm
    experts = jnp.arange(N_EXPERTS, dtype=jnp.int32)
    uses = jnp.any(idx4[:, :, None] == experts[None, None, :], axis=1).astype(jnp.int32)
    before = jnp.cumsum(uses, axis=0) - uses
    rank = jnp.take_along_axis(before, idx4, axis=1)
    counts = jnp.sum(uses, axis=0)
    padded = (counts + tm - 1) // tm * tm
    gend = jnp.cumsum(padded)
    gstart = gend - padded
    pos = (gstart[idx4] + rank).reshape(n_pairs).astype(jnp.int32)
    tile_start = jnp.arange(n_tiles, dtype=jnp.int32) * tm
    tile_expert = jnp.minimum(jnp.sum((gend[None, :] <= tile_start[:, None]).astype(jnp.int32), axis=1),
                              N_EXPERTS - 1)
    tile_rows = jnp.clip(counts[tile_expert] - (tile_start - gstart[tile_expert]), 0, tm).astype(jnp.int32)
    tile_first = (tile_start == gstart[tile_expert]).astype(jnp.int32)
    token = jnp.arange(n_pairs, dtype=jnp.int32) // TOP_K
    src = jnp.zeros((n_rows,), jnp.int32).at[pos].set(token, unique_indices=True)
    return pos, src.reshape(n_tiles, 1, tm), tile_expert.astype(jnp.int32), tile_rows, tile_first


def _gather_copy(h_hbm, buf, sem, slot):
    return pltpu.make_async_copy(h_hbm.at[pl.ds(0, buf.shape[1])], buf.at[slot], sem.at[slot])


def _ffn_kernel(te_ref, tr_ref, tf_ref, src0_ref, src1_ref, h_hbm, wgu_ref, bgu_ref, wd_ref, bd_ref, y_ref,
                buf, sem, wgu_bf, wd_bf):
    i = pl.program_id(0)
    n = pl.num_programs(0)
    tm = buf.shape[1]
    slot = i % 2

    def issue(src_ref, dst_slot):
        for r in range(tm):
            pltpu.make_async_copy(h_hbm.at[pl.ds(src_ref[0, 0, r], 1)],
                                  buf.at[dst_slot, pl.ds(r, 1)], sem.at[dst_slot]).start()

    @pl.when((i == 0) & (tr_ref[0] > 0))
    def _():
        issue(src0_ref, 0)

    nxt = jnp.minimum(i + 1, n - 1)
    prefetch = (i + 1 < n) & (tr_ref[nxt] > 0)
    for s in range(2):
        @pl.when(prefetch & (slot == s))
        def _():
            issue(src1_ref, 1 - s)

    @pl.when((tr_ref[i] > 0) & (tf_ref[i] > 0))
    def _():
        wgu_bf[...] = wgu_ref[0].astype(BF16)
        wd_bf[...] = wd_ref[0].astype(BF16)

    @pl.when(tr_ref[i] > 0)
    def _():
        _gather_copy(h_hbm, buf, sem, slot).wait()
        x = buf[slot].astype(BF16)
        gu = jnp.dot(x, wgu_bf[...], preferred_element_type=F32) + bgu_ref[0]
        gate = jnp.minimum(gu[:, :D_EXPERT], SWIGLU_LIMIT)
        up = jnp.clip(gu[:, D_EXPERT:], -SWIGLU_LIMIT, SWIGLU_LIMIT)
        act = (up + 1.0) * (gate * jax.nn.sigmoid(SWIGLU_ALPHA * gate))
        y_ref[...] = jnp.dot(act.astype(BF16), wd_bf[...], preferred_element_type=F32) + bd_ref[0]

    @pl.when(tr_ref[i] == 0)
    def _():
        y_ref[...] = jnp.zeros(y_ref.shape, y_ref.dtype)


def _expert_ffn(h, src, tile_expert, tile_rows, tile_first, w_gu, b_gu, w_down, b_down):
    n_tok, d = h.shape
    n_tiles, _, tm = src.shape
    grid_spec = pltpu.PrefetchScalarGridSpec(
        num_scalar_prefetch=3,
        grid=(n_tiles,),
        in_specs=[
            pl.BlockSpec((1, 1, tm), lambda i, te, tr, tf: (i, 0, 0), memory_space=pltpu.SMEM),
            pl.BlockSpec((1, 1, tm), lambda i, te, tr, tf: (jnp.minimum(i + 1, n_tiles - 1), 0, 0),
                         memory_space=pltpu.SMEM),
            pl.BlockSpec(memory_space=pl.ANY),
            pl.BlockSpec((1, d, 2 * D_EXPERT), lambda i, te, tr, tf: (te[i], 0, 0)),
            pl.BlockSpec((1, 1, 2 * D_EXPERT), lambda i, te, tr, tf: (te[i], 0, 0)),
            pl.BlockSpec((1, D_EXPERT, d), lambda i, te, tr, tf: (te[i], 0, 0)),
            pl.BlockSpec((1, 1, d), lambda i, te, tr, tf: (te[i], 0, 0)),
        ],
        out_specs=pl.BlockSpec((tm, d), lambda i, te, tr, tf: (i, 0)),
        scratch_shapes=[pltpu.VMEM((2, tm, d), F32), pltpu.SemaphoreType.DMA((2,)),
                        pltpu.VMEM((d, 2 * D_EXPERT), BF16), pltpu.VMEM((D_EXPERT, d), BF16)],
    )
    return pl.pallas_call(
        _ffn_kernel,
        grid_spec=grid_spec,
        out_shape=jax.ShapeDtypeStruct((n_tiles * tm, d), F32),
        compiler_params=_params(("arbitrary",)),
        name="expert_ffn",
    )(tile_expert, tile_rows, tile_first, src, src, h, w_gu, b_gu.reshape(N_EXPERTS, 1, -1),
      w_down, b_down.reshape(N_EXPERTS, 1, -1))


def _combine_kernel(pos0_ref, pos1_ref, y_hbm, gate_ref, x_ref, m_ref, o_ref, buf, sem, *, i_gate):
    i = pl.program_id(0)
    n = pl.num_programs(0)
    tc = buf.shape[2]
    slot = i % 2

    def issue(pos_ref, dst_slot):
        for r in range(tc):
            for k in range(TOP_K):
                pltpu.make_async_copy(y_hbm.at[pl.ds(pos_ref[0, 0, r * TOP_K + k], 1)],
                                      buf.at[dst_slot, k, pl.ds(r, 1)], sem.at[dst_slot]).start()

    @pl.when(i == 0)
    def _():
        issue(pos0_ref, 0)

    for s in range(2):
        @pl.when((i + 1 < n) & (slot == s))
        def _():
            issue(pos1_ref, 1 - s)

    for k in range(TOP_K):
        pltpu.make_async_copy(y_hbm.at[pl.ds(0, tc)], buf.at[slot, k], sem.at[slot]).wait()
    g = gate_ref[...]
    y = g[:, 0:1] * buf[slot, 0]
    for k in range(1, TOP_K):
        y = y + g[:, k:k + 1] * buf[slot, k]
    o_ref[...] = x_ref[...] + m_ref[0, 0, i_gate:i_gate + 1] * y


def _combine(y, pos, gate, x, mods, i_gate, tiles_per_sample):
    n_tok, d = x.shape
    tc = ROW_TILE
    n_steps = n_tok // tc
    pos3 = pos.reshape(n_steps, 1, tc * TOP_K)

    def mod_map(i):
        return (i // tiles_per_sample, jnp.minimum(i % tiles_per_sample, 1), 0, 0)

    return pl.pallas_call(
        functools.partial(_combine_kernel, i_gate=i_gate),
        grid=(n_steps,),
        in_specs=[
            pl.BlockSpec((1, 1, tc * TOP_K), lambda i: (i, 0, 0), memory_space=pltpu.SMEM),
            pl.BlockSpec((1, 1, tc * TOP_K), lambda i: (jnp.minimum(i + 1, n_steps - 1), 0, 0),
                         memory_space=pltpu.SMEM),
            pl.BlockSpec(memory_space=pl.ANY),
            pl.BlockSpec((tc, LANES), lambda i: (i, 0)),
            pl.BlockSpec((tc, d), lambda i: (i, 0)),
            pl.BlockSpec((1, 1, 6, d), mod_map),
        ],
        out_specs=pl.BlockSpec((tc, d), lambda i: (i, 0)),
        out_shape=jax.ShapeDtypeStruct((n_tok, d), F32),
        scratch_shapes=[pltpu.VMEM((2, TOP_K, tc, d), F32), pltpu.SemaphoreType.DMA((2,))],
        compiler_params=_params(("arbitrary",)),
        name="moe_combine",
    )(pos3, pos3, y, gate, x, mods)


def _moe_block(x, g, mods, w_router, b_router, w_gu, b_gu, w_down, b_down):
    bsz, nt, d = x.shape
    h, idx, gate = _rms_router(x, g, mods, w_router, b_router, i_shift=3, i_scale=4)
    n_tok = bsz * nt
    pos, src, tile_expert, tile_rows, tile_first = _route(idx.reshape(n_tok, LANES)[:, :TOP_K])
    y = _expert_ffn(h.reshape(n_tok, d), src, tile_expert, tile_rows, tile_first, w_gu, b_gu, w_down, b_down)
    out = _combine(y, pos, gate.reshape(n_tok, LANES), x.reshape(n_tok, d), mods, 5, nt // ROW_TILE)
    return out.reshape(bsz, nt, d)


def kernel(x, c, ctx, c_ctx, ada_w, ada_b, norm_attn_g, norm_ffn_g, na_w_qkv, na_rpb, na_w_o, sw_w_qkv, sw_sinks, sw_w_o, diff_w_qkv, diff_lam_q1, diff_lam_k1, diff_lam_q2, diff_lam_k2, diff_sub_g, diff_w_o, router_w, router_b, exp_w_gu, exp_b_gu, exp_w_down, exp_b_down, final_g):
    bsz, n_lat, d = x.shape
    n_ctx = ctx.shape[1]
    assert d == D_MODEL and n_ctx == CTX_LEN and n_lat % GRID_W == 0 and bsz < ADA_ROWS
    assert n_lat % ROW_TILE == 0

    cond = jnp.concatenate([c, c_ctx[None, :], jnp.zeros((ADA_ROWS - bsz - 1, d), c.dtype)], axis=0)
    ada = _ada_all(jax.nn.silu(cond).astype(BF16), ada_w, ada_b)
    rope = _rope_tables(n_ctx, n_lat)

    xs = jnp.concatenate([ctx, x], axis=1)
    for i in range(DEPTH):
        m_lat = ada[i, :bsz].reshape(bsz, 6, d)
        m_ctx = jnp.broadcast_to(ada[i, bsz].reshape(1, 6, d), (bsz, 6, d))
        mods = jnp.stack([m_ctx, m_lat], axis=1)
        h = _rms_mod(xs, norm_attn_g[i], mods, i_shift=0, i_scale=1)
        j = i // N_MIXERS
        kind = i % N_MIXERS
        if kind == 0:
            qkv = _qkv_proj(h, na_w_qkv[j], d, 0, None)
            o = _na_attention(qkv, _na_bias_table(na_rpb[j], n_lat // GRID_W), n_ctx)
            w_o = na_w_o[j]
        elif kind == 1:
            n_kv = N_KV_HEADS * HEAD_DIM
            qkv = _qkv_proj(h, sw_w_qkv[j], d, d + n_kv, rope)
            o = _sw_attention(qkv, sw_sinks[j], n_ctx)
            w_o = sw_w_o[j]
        else:
            lambda_init = 0.8 - 0.6 * math.exp(-0.3 * i)
            lam = (jnp.exp(jnp.sum(diff_lam_q1[j].astype(F32) * diff_lam_k1[j].astype(F32)))
                   - jnp.exp(jnp.sum(diff_lam_q2[j].astype(F32) * diff_lam_k2[j].astype(F32))) + lambda_init)
            qkv = _qkv_proj(h, diff_w_qkv[j], d, 2 * d, rope, q_scale=HEAD_DIM ** -0.5 * LOG2_E)
            o = _diff_attention(qkv, lam, diff_sub_g[j], lambda_init, n_ctx)
            w_o = diff_w_o[j]
        xs = _out_proj(o, w_o, xs, mods, n_ctx, i_gate=2)
        xs = _moe_block(xs, norm_ffn_g[i], mods, router_w[i], router_b[i], exp_w_gu[i], exp_b_gu[i],
                        exp_w_down[i], exp_b_down[i])
    return _final_norm(xs, final_g, n_ctx)
```

```python
import functools
import math

import jax
import jax.numpy as jnp
from jax import lax
from jax.experimental import pallas as pl
from jax.experimental.pallas import tpu as pltpu

D_MODEL = 2048
DEPTH = 4
CTX_LEN = 256
GRID_W = 64
N_MIXERS = 3
HEAD_DIM = 64
N_HEADS = D_MODEL // HEAD_DIM
N_KV_HEADS = N_HEADS // 8
GQA_GROUP = N_HEADS // N_KV_HEADS
N_DIFF_HEADS = N_HEADS // 2
NA_ROWS = 8
NA_COLS = 16
NA_QROWS = 4
NA_WIN = 12
SW_WINDOW = 128
Q_BLOCK = 128
DIFF_QBLOCK = 256
SW_STACK = 4
ROPE_BASE = 10000.0
N_EXPERTS = 32
TOP_K = 4
D_EXPERT = D_MODEL // 4
SWIGLU_LIMIT = 7.0
SWIGLU_ALPHA = 1.702
NORM_EPS = 1e-6
NEG_INF = -1e30
LOG2_E = 1.4426950408889634

LANES = 128
VMEM_LIMIT_BYTES = 52 * 1024 * 1024
ROW_TILE = 256
MM_ROW_TILES = (1088, 1024, 640, 512, 256)
EXPERT_TILE = 256
ADA_ROWS = 16

F32 = jnp.float32
BF16 = jnp.bfloat16


def _params(semantics):
    return pltpu.CompilerParams(dimension_semantics=semantics, vmem_limit_bytes=VMEM_LIMIT_BYTES)


def _mm_row_tile(nt):
    return next(t for t in MM_ROW_TILES if nt % t == 0)


def _dot_nt(a, b):
    return lax.dot_general(a, b, (((1,), (1,)), ((), ())), preferred_element_type=F32)


def _ada_kernel(s_ref, w_ref, b_ref, o_ref):
    o_ref[0] = jnp.dot(s_ref[...], w_ref[0].astype(BF16), preferred_element_type=F32) + b_ref[0]


def _ada_all(s, ada_w, ada_b):
    depth, d, n = ada_w.shape
    tn = 1024
    return pl.pallas_call(
        _ada_kernel,
        grid=(depth, n // tn),
        in_specs=[
            pl.BlockSpec((ADA_ROWS, d), lambda i, j: (0, 0)),
            pl.BlockSpec((1, d, tn), lambda i, j: (i, 0, j)),
            pl.BlockSpec((1, 1, tn), lambda i, j: (i, 0, j)),
        ],
        out_specs=pl.BlockSpec((1, ADA_ROWS, tn), lambda i, j: (i, 0, j)),
        out_shape=jax.ShapeDtypeStruct((depth, ADA_ROWS, n), F32),
        compiler_params=_params(("arbitrary", "arbitrary")),
        name="ada_ln",
    )(s, ada_w, ada_b.reshape(depth, 1, n))


def _normed(x, g, eps):
    ms = jnp.mean(x * x, axis=-1, keepdims=True)
    return x * lax.rsqrt(ms + eps) * g


def _rms_mod_kernel(x_ref, g_ref, m_ref, o_ref, *, i_shift, i_scale):
    m = m_ref[0, 0]
    y = _normed(x_ref[0], g_ref[...], NORM_EPS)
    o_ref[0] = (y * (1.0 + m[i_scale:i_scale + 1]) + m[i_shift:i_shift + 1]).astype(o_ref.dtype)


def _mods_spec():
    return pl.BlockSpec((1, 1, 6, D_MODEL), lambda b, j: (b, jnp.minimum(j, 1), 0, 0))


def _rms_mod(x, g, mods, i_shift, i_scale):
    bsz, nt, d = x.shape
    return pl.pallas_call(
        functools.partial(_rms_mod_kernel, i_shift=i_shift, i_scale=i_scale),
        grid=(bsz, nt // ROW_TILE),
        in_specs=[
            pl.BlockSpec((1, ROW_TILE, d), lambda b, j: (b, j, 0)),
            pl.BlockSpec((1, d), lambda b, j: (0, 0)),
            _mods_spec(),
        ],
        out_specs=pl.BlockSpec((1, ROW_TILE, d), lambda b, j: (b, j, 0)),
        out_shape=jax.ShapeDtypeStruct((bsz, nt, d), BF16),
        compiler_params=_params(("arbitrary", "arbitrary")),
        name="rms_mod",
    )(x, g.reshape(1, d), mods)


def _rms_router_kernel(x_ref, g_ref, m_ref, wr_ref, br_ref, h_ref, gate_ref, meta_ref, cnt_ref, *,
                       i_shift, i_scale):
    m = m_ref[0, 0]
    y = _normed(x_ref[0], g_ref[...], NORM_EPS)
    h = y * (1.0 + m[i_scale:i_scale + 1]) + m[i_shift:i_shift + 1]
    h_ref[0] = h
    h_hi = h.astype(BF16)
    h_lo = (h - h_hi.astype(F32)).astype(BF16)
    part = (jnp.dot(h_hi, wr_ref[...], preferred_element_type=F32)
            + jnp.dot(h_lo, wr_ref[...], preferred_element_type=F32))
    logits = part + pltpu.roll(part, LANES - N_EXPERTS, axis=1) + br_ref[...]
    lane = lax.broadcasted_iota(jnp.int32, logits.shape, 1)
    logits = jnp.where(lane < N_EXPERTS, logits, NEG_INF)
    vals, ids = [], []
    for _ in range(TOP_K):
        mx = jnp.max(logits, axis=-1, keepdims=True)
        ik = jnp.min(jnp.where(logits == mx, lane, LANES), axis=-1, keepdims=True)
        vals.append(mx)
        ids.append(ik)
        logits = jnp.where(lane == ik, -jnp.inf, logits)
    es = [jnp.exp(v - vals[0]) for v in vals]
    inv = 1.0 / (es[0] + es[1] + es[2] + es[3])
    gate = jnp.zeros(lane.shape, F32)
    for k in range(TOP_K):
        gate = jnp.where(lane == k, es[k] * inv, gate)
    gate_ref[0] = gate

    @pl.when((pl.program_id(0) == 0) & (pl.program_id(1) == 0))
    def _():
        cnt_ref[...] = jnp.zeros(cnt_ref.shape, cnt_ref.dtype)

    uses = jnp.zeros(lane.shape, F32)
    for k in range(TOP_K):
        uses = jnp.where(lane == ids[k], 1.0, uses)
    tr = uses.shape[0]
    tri = (lax.broadcasted_iota(jnp.int32, (tr, tr), 0) > lax.broadcasted_iota(jnp.int32, (tr, tr), 1))
    before = cnt_ref[...] + jnp.dot(jnp.where(tri, 1.0, 0.0).astype(BF16), uses.astype(BF16),
                                    preferred_element_type=F32)
    cnt_ref[...] += jnp.sum(uses, axis=0, keepdims=True)
    meta = jnp.zeros(lane.shape, jnp.int32)
    for k in range(TOP_K):
        rank = jnp.sum(jnp.where(lane == ids[k], before, 0.0), axis=-1, keepdims=True)
        meta = jnp.where(lane == k, ids[k], meta)
        meta = jnp.where(lane == TOP_K + k, rank.astype(jnp.int32), meta)
    meta_ref[0] = jnp.transpose(meta)[0:2 * TOP_K, :]


def _rms_router(x, g, mods, w_router, b_router, i_shift, i_scale):
    bsz, nt, d = x.shape
    w_hi = w_router.astype(BF16)
    w_lo = (w_router - w_hi.astype(F32)).astype(BF16)
    wr = jnp.zeros((d, LANES), BF16).at[:, :N_EXPERTS].set(w_hi).at[:, N_EXPERTS:2 * N_EXPERTS].set(w_lo)
    br = jnp.zeros((1, LANES), F32).at[0, :N_EXPERTS].set(b_router)
    row = pl.BlockSpec((1, ROW_TILE, d), lambda b, j: (b, j, 0))
    lanes = pl.BlockSpec((1, ROW_TILE, LANES), lambda b, j: (b, j, 0))
    return pl.pallas_call(
        functools.partial(_rms_router_kernel, i_shift=i_shift, i_scale=i_scale),
        grid=(bsz, nt // ROW_TILE),
        in_specs=[
            row,
            pl.BlockSpec((1, d), lambda b, j: (0, 0)),
            _mods_spec(),
            pl.BlockSpec((d, LANES), lambda b, j: (0, 0)),
            pl.BlockSpec((1, LANES), lambda b, j: (0, 0)),
        ],
        out_specs=[row, lanes,
                   pl.BlockSpec((1, 2 * TOP_K, ROW_TILE), lambda b, j: (b, 0, j)),
                   pl.BlockSpec((1, LANES), lambda b, j: (0, 0))],
        out_shape=[
            jax.ShapeDtypeStruct((bsz, nt, d), F32),
            jax.ShapeDtypeStruct((bsz, nt, LANES), F32),
            jax.ShapeDtypeStruct((bsz, 2 * TOP_K, nt), jnp.int32),
            jax.ShapeDtypeStruct((1, LANES), F32),
        ],
        compiler_params=_params(("arbitrary", "arbitrary")),
        name="rms_router",
    )(x, g.reshape(1, d), mods, wr, br)


def _final_norm_kernel(x_ref, g_ref, o_ref):
    o_ref[0] = _normed(x_ref[0], g_ref[...], NORM_EPS)


def _final_norm(x, g, n_ctx):
    bsz, nt, d = x.shape
    skip = n_ctx // ROW_TILE
    return pl.pallas_call(
        _final_norm_kernel,
        grid=(bsz, (nt - n_ctx) // ROW_TILE),
        in_specs=[
            pl.BlockSpec((1, ROW_TILE, d), lambda b, j: (b, j + skip, 0)),
            pl.BlockSpec((1, d), lambda b, j: (0, 0)),
        ],
        out_specs=pl.BlockSpec((1, ROW_TILE, d), lambda b, j: (b, j, 0)),
        out_shape=jax.ShapeDtypeStruct((bsz, nt - n_ctx, d), F32),
        compiler_params=_params(("arbitrary", "arbitrary")),
        name="final_norm",
    )(x, g.reshape(1, d))


def _rope_tables(n_ctx, n_lat):
    half = HEAD_DIM // 2
    nf = half // 2
    t = jnp.arange(n_lat)
    inv_freq = ROPE_BASE ** (-jnp.arange(nf, dtype=F32) / nf)
    lane = jnp.arange(LANES)
    d = lane % HEAD_DIM
    pos = jnp.where((d < half)[None, :], (t // GRID_W)[:, None], (t % GRID_W)[:, None]).astype(F32)
    ang = pos * inv_freq[d % nf][None, :]
    sign = jnp.where((d % half) < nf, -1.0, 1.0)[None, :]
    cos = jnp.concatenate([jnp.ones((n_ctx, LANES), F32), jnp.cos(ang)], axis=0)
    sin = jnp.concatenate([jnp.zeros((n_ctx, LANES), F32), jnp.sin(ang) * sign], axis=0)
    return cos, sin


def _qkv_kernel(x_ref, w_ref, *rest, n_q_chunks, n_rope_chunks, q_scale):
    if n_rope_chunks:
        cos_ref, sin_ref, o_ref = rest
    else:
        (o_ref,) = rest
    acc = jnp.dot(x_ref[0], w_ref[...], preferred_element_type=F32)
    chunks = acc.shape[1] // LANES
    j = pl.program_id(2)
    nf = HEAD_DIM // 4
    for c in range(chunks):
        gc = j * chunks + c
        a = acc[:, c * LANES:(c + 1) * LANES]
        if n_rope_chunks:
            lane = lax.broadcasted_iota(jnp.int32, a.shape, 1)
            swapped = jnp.where((lane % (2 * nf)) < nf,
                                pltpu.roll(a, LANES - nf, axis=1), pltpu.roll(a, nf, axis=1))
            use = gc < n_rope_chunks
            a = a * jnp.where(use, cos_ref[...], 1.0) + swapped * jnp.where(use, sin_ref[...], 0.0)
        a = a * jnp.where(gc < n_q_chunks, q_scale, 1.0)
        o_ref[0, :, c * LANES:(c + 1) * LANES] = a.astype(o_ref.dtype)


def _qkv_proj(h, w, n_q_cols, n_rope_cols, rope, q_scale=HEAD_DIM ** -0.5):
    bsz, nt, d = h.shape
    n = w.shape[1]
    tn = 512
    tm = _mm_row_tile(nt)
    in_specs = [
        pl.BlockSpec((1, tm, d), lambda b, i, j: (b, i, 0)),
        pl.BlockSpec((d, tn), lambda b, i, j: (0, j)),
    ]
    args = [h, w.astype(BF16)]
    if n_rope_cols:
        in_specs += [pl.BlockSpec((tm, LANES), lambda b, i, j: (i, 0))] * 2
        args += list(rope)
    return pl.pallas_call(
        functools.partial(_qkv_kernel, n_q_chunks=n_q_cols // LANES, n_rope_chunks=n_rope_cols // LANES,
                          q_scale=q_scale),
        grid=(bsz, nt // tm, n // tn),
        in_specs=in_specs,
        out_specs=pl.BlockSpec((1, tm, tn), lambda b, i, j: (b, i, j)),
        out_shape=jax.ShapeDtypeStruct((bsz, nt, n), BF16),
        compiler_params=_params(("arbitrary", "arbitrary", "arbitrary")),
        name="qkv_proj",
    )(*args)


def _out_proj_kernel(o_ref, w_ref, x_ref, m_ref, y_ref, *, n_ctx, i_gate):
    acc = jnp.dot(o_ref[0], w_ref[...], preferred_element_type=F32)
    tm = acc.shape[0]
    row = pl.program_id(1) * tm + lax.broadcasted_iota(jnp.int32, (tm, 1), 0)
    m = m_ref[0]
    gate = jnp.where(row < n_ctx, m[0, i_gate:i_gate + 1], m[1, i_gate:i_gate + 1])
    y_ref[0] = x_ref[0] + gate * acc


def _out_proj(o, w, x, mods, n_ctx, i_gate):
    bsz, nt, d = o.shape
    n = w.shape[1]
    tn = 512
    tm = _mm_row_tile(nt)
    return pl.pallas_call(
        functools.partial(_out_proj_kernel, n_ctx=n_ctx, i_gate=i_gate),
        grid=(bsz, nt // tm, n // tn),
        in_specs=[
            pl.BlockSpec((1, tm, d), lambda b, i, j: (b, i, 0)),
            pl.BlockSpec((d, tn), lambda b, i, j: (0, j)),
            pl.BlockSpec((1, tm, tn), lambda b, i, j: (b, i, j)),
            pl.BlockSpec((1, 2, 6, tn), lambda b, i, j: (b, 0, 0, j)),
        ],
        out_specs=pl.BlockSpec((1, tm, tn), lambda b, i, j: (b, i, j)),
        out_shape=jax.ShapeDtypeStruct((bsz, nt, n), F32),
        compiler_params=_params(("arbitrary", "arbitrary", "arbitrary")),
        name="out_proj",
    )(o, w.astype(BF16), x, mods)


def _softmax_pv(s, v, extra=None, base2=False):
    exp = jnp.exp2 if base2 else jnp.exp
    mx = jnp.max(s, axis=-1, keepdims=True)
    if extra is not None:
        mx = jnp.maximum(mx, extra)
    e = exp(s - mx)
    den = jnp.sum(e, axis=-1, keepdims=True)
    if extra is not None:
        den = den + exp(extra - mx)
    return jnp.dot(e.astype(BF16), v, preferred_element_type=F32) / den


def _split_heads(x):
    lo = lax.broadcasted_iota(jnp.int32, x.shape, 1) < HEAD_DIM
    zero = jnp.zeros_like(x)
    return jnp.concatenate([jnp.where(lo, x, zero), jnp.where(lo, zero, x)], axis=0)


def _merge_heads(o):
    n = o.shape[0] // 2
    lo = lax.broadcasted_iota(jnp.int32, (n, o.shape[1]), 1) < HEAD_DIM
    return jnp.where(lo, o[:n], o[n:])


def _na_block_geometry(rows):
    n_blk = rows // NA_QROWS
    r0 = [j * NA_QROWS for j in range(n_blk)]
    ks = [min(max(r - NA_ROWS // 2, 0), rows - NA_WIN) for r in r0]
    cfg = [0 if j == 0 else (2 if j == n_blk - 1 else 1) for j in range(n_blk)]
    return r0, ks, cfg


def _na_bias_table(rpb, rows):
    r0s, kss, cfgs = _na_block_geometry(rows)
    assert all(r0s[j] - kss[j] == r0s[1] - kss[1] for j in range(len(cfgs)) if cfgs[j] == 1)
    rep = {c: j for j, c in reversed(list(enumerate(cfgs)))}
    qc = jnp.arange(GRID_W)[:, None]
    kc = jnp.arange(GRID_W)[None, :]
    win = jnp.clip(qc - NA_COLS // 2, 0, GRID_W - NA_COLS)
    col_ok = (kc >= win) & (kc < win + NA_COLS)
    col_off = jnp.clip(kc - qc + NA_COLS - 1, 0, 2 * NA_COLS - 2)
    tabs = []
    for c in range(3):
        r = r0s[rep[c]] + jnp.arange(NA_QROWS)[:, None]
        kr = kss[rep[c]] + jnp.arange(NA_WIN)[None, :]
        rs = jnp.clip(r - NA_ROWS // 2, 0, rows - NA_ROWS)
        row_ok = (kr >= rs) & (kr < rs + NA_ROWS)
        row_off = jnp.clip(kr - r + NA_ROWS - 1, 0, 2 * NA_ROWS - 2)
        t = rpb[:, row_off][:, :, :, col_off]
        ok = row_ok[:, :, None, None] & col_ok[None, None]
        t = jnp.where(ok[None], t, NEG_INF).transpose(0, 1, 3, 2, 4)
        tabs.append(t.reshape(rpb.shape[0], NA_QROWS * GRID_W, NA_WIN * GRID_W))
    h = rpb.shape[0]
    tab = jnp.stack(tabs, axis=1).reshape(h // 2, 2, 3, NA_QROWS * GRID_W, NA_WIN * GRID_W)
    tab = tab.transpose(0, 2, 1, 3, 4)
    return tab.reshape(h // 2, 3, 2 * NA_QROWS * GRID_W, NA_WIN * GRID_W).astype(F32)


def _na_kernel(q_ref, k_ref, v_ref, bias_ref, o_ref, *, n_ctx, rows):
    nq = NA_QROWS * GRID_W
    nwin = NA_WIN * GRID_W
    n_blk = rows // NA_QROWS
    s = _dot_nt(_split_heads(q_ref[0, 0:n_ctx, :]), k_ref[0, 0:n_ctx, :])
    o_ref[0, 0:n_ctx, :] = _merge_heads(_softmax_pv(s, v_ref[0, 0:n_ctx, :])).astype(o_ref.dtype)

    def block(j, carry):
        r0 = j * NA_QROWS
        ks = jnp.clip(r0 - NA_ROWS // 2, 0, rows - NA_WIN)
        cfg = jnp.where(j == 0, 0, jnp.where(j == n_blk - 1, 2, 1))
        q0 = pl.multiple_of(n_ctx + r0 * GRID_W, GRID_W)
        k0 = pl.multiple_of(n_ctx + ks * GRID_W, GRID_W)
        kk = jnp.concatenate([k_ref[0, 0:n_ctx, :], k_ref[0, pl.ds(k0, nwin), :]], axis=0)
        vv = jnp.concatenate([v_ref[0, 0:n_ctx, :], v_ref[0, pl.ds(k0, nwin), :]], axis=0)
        s = _dot_nt(_split_heads(q_ref[0, pl.ds(q0, nq), :]), kk)
        s = jnp.concatenate([s[:, :n_ctx], s[:, n_ctx:] + bias_ref[0, cfg]], axis=1)
        o_ref[0, pl.ds(q0, nq), :] = _merge_heads(_softmax_pv(s, vv)).astype(o_ref.dtype)
        return carry

    lax.fori_loop(0, n_blk, block, 0, unroll=4)


def _na_attention(qkv, bias_tab, n_ctx):
    bsz, nt, _ = qkv.shape
    rows = (nt - n_ctx) // GRID_W
    n_hp = N_HEADS // 2
    blk = lambda off: pl.BlockSpec((1, nt, LANES), lambda p, b: (b, 0, p + off))
    return pl.pallas_call(
        functools.partial(_na_kernel, n_ctx=n_ctx, rows=rows),
        grid=(n_hp, bsz),
        in_specs=[
            blk(0), blk(n_hp), blk(2 * n_hp),
            pl.BlockSpec((1,) + bias_tab.shape[1:], lambda p, b: (p, 0, 0, 0)),
        ],
        out_specs=pl.BlockSpec((1, nt, LANES), lambda p, b: (b, 0, p)),
        out_shape=jax.ShapeDtypeStruct((bsz, nt, D_MODEL), BF16),
        compiler_params=_params(("arbitrary", "arbitrary")),
        name="na_attention",
    )(qkv, qkv, qkv, bias_tab)


def _sw_kernel(sink_ref, q_ref, k_ref, v_ref, o_ref, *, n_ctx, n_lat):
    hd = HEAD_DIM
    qb = Q_BLOCK
    n_blk = n_lat // qb
    p = pl.program_id(1)
    jl = pl.program_id(2) - n_ctx // qb
    starts, valids = [], []
    for c in range(3):
        blk = jl - 1 + c
        starts.append(pl.multiple_of(n_ctx + qb * jnp.clip(blk, 0, n_blk - 1), qb))
        valids.append((blk >= 0) & (blk < n_blk) & (jl >= 0))
    iq = lax.broadcasted_iota(jnp.int32, (qb, qb), 0)
    ik = lax.broadcasted_iota(jnp.int32, (qb, qb), 1)
    oks = [valids[c] & (jnp.abs(iq - ik - (c - 1) * qb) <= SW_WINDOW) for c in range(3)]
    ok = jnp.concatenate(oks + [jnp.full((qb, n_ctx), True)], axis=1)
    k_all = jnp.concatenate([k_ref[0, pl.ds(s0, qb), :] for s0 in starts] + [k_ref[0, 0:n_ctx, :]], axis=0)
    v_all = jnp.concatenate([v_ref[0, pl.ds(s0, qb), :] for s0 in starts] + [v_ref[0, 0:n_ctx, :]], axis=0)
    gw = GQA_GROUP * hd
    for g in range(2):
        kg = jnp.concatenate([k_all[:, g * hd:(g + 1) * hd]] * 2, axis=1)
        vg = jnp.concatenate([v_all[:, g * hd:(g + 1) * hd]] * 2, axis=1)
        for c0 in range(0, GQA_GROUP // 2, SW_STACK // 2):
            cols = [g * gw + c * LANES for c in range(c0, c0 + SW_STACK // 2)]
            qs = jnp.concatenate([_split_heads(q_ref[0, :, c:c + LANES]) for c in cols], axis=0)
            s = _dot_nt(qs, kg)
            s = jnp.concatenate([jnp.where(ok, s[i * qb:(i + 1) * qb], NEG_INF) for i in range(SW_STACK)], axis=0)
            sink = jnp.concatenate([jnp.full((qb, 1), sink_ref[(p * 2 + g) * GQA_GROUP + 2 * c0 + i], F32)
                                    for i in range(SW_STACK)], axis=0)
            o = _softmax_pv(s, vg, extra=sink)
            for n, c in enumerate(cols):
                o_ref[0, :, c:c + LANES] = _merge_heads(o[2 * n * qb:(2 * n + 2) * qb]).astype(o_ref.dtype)


def _sw_attention(qkv, sinks, n_ctx):
    bsz, nt, _ = qkv.shape
    n_lat = nt - n_ctx
    qw = 2 * GQA_GROUP * HEAD_DIM
    k_blk = D_MODEL // LANES
    v_blk = k_blk + N_KV_HEADS * HEAD_DIM // LANES
    grid_spec = pltpu.PrefetchScalarGridSpec(
        num_scalar_prefetch=1,
        grid=(bsz, N_KV_HEADS // 2, nt // Q_BLOCK),
        in_specs=[
            pl.BlockSpec((1, Q_BLOCK, qw), lambda b, p, j, s: (b, j, p)),
            pl.BlockSpec((1, nt, LANES), lambda b, p, j, s: (b, 0, k_blk + p)),
            pl.BlockSpec((1, nt, LANES), lambda b, p, j, s: (b, 0, v_blk + p)),
        ],
        out_specs=pl.BlockSpec((1, Q_BLOCK, qw), lambda b, p, j, s: (b, j, p)),
    )
    return pl.pallas_call(
        functools.partial(_sw_kernel, n_ctx=n_ctx, n_lat=n_lat),
        grid_spec=grid_spec,
        out_shape=jax.ShapeDtypeStruct((bsz, nt, D_MODEL), BF16),
        compiler_params=_params(("arbitrary", "arbitrary", "arbitrary")),
        name="sw_attention",
    )(sinks.astype(F32), qkv, qkv, qkv)


def _diff_kernel(lam_ref, q_ref, k_ref, v_ref, g_ref, o_ref, *, n_ctx, lambda_init):
    lam = lam_ref[0]
    nt = k_ref.shape[1]
    tq = DIFF_QBLOCK

    def attend(q, n_keys):
        n = q.shape[0]
        qs = _split_heads(q)
        k = k_ref[0, 0:n_keys, :]
        v = v_ref[0, 0:n_keys, :]
        outs = [_softmax_pv(_dot_nt(qs[i * n:(i + 1) * n], k), v, base2=True) for i in range(2)]
        o = outs[0] - lam * outs[1]
        return (_normed(o, g_ref[...], 1e-5) * (1.0 - lambda_init)).astype(o_ref.dtype)

    o_ref[0, 0:n_ctx, :] = attend(q_ref[0, 0:n_ctx, :], n_ctx)

    def block(j, carry):
        r0 = pl.multiple_of(n_ctx + j * tq, LANES)
        o_ref[0, pl.ds(r0, tq), :] = attend(q_ref[0, pl.ds(r0, tq), :], nt)
        return carry

    lax.fori_loop(0, (nt - n_ctx) // tq, block, 0, unroll=4)


def _diff_attention(qkv, lam, sub_g, lambda_init, n_ctx):
    bsz, nt, _ = qkv.shape
    nh = N_DIFF_HEADS
    assert (nt - n_ctx) % DIFF_QBLOCK == 0
    grid_spec = pltpu.PrefetchScalarGridSpec(
        num_scalar_prefetch=1,
        grid=(bsz, nh),
        in_specs=[
            pl.BlockSpec((1, nt, LANES), lambda b, h, s: (b, 0, h)),
            pl.BlockSpec((1, nt, LANES), lambda b, h, s: (b, 0, nh + h)),
            pl.BlockSpec((1, nt, LANES), lambda b, h, s: (b, 0, 2 * nh + h)),
            pl.BlockSpec((1, LANES), lambda b, h, s: (0, 0)),
        ],
        out_specs=pl.BlockSpec((1, nt, LANES), lambda b, h, s: (b, 0, h)),
    )
    return pl.pallas_call(
        functools.partial(_diff_kernel, n_ctx=n_ctx, lambda_init=lambda_init),
        grid_spec=grid_spec,
        out_shape=jax.ShapeDtypeStruct((bsz, nt, D_MODEL), BF16),
        compiler_params=_params(("arbitrary", "arbitrary")),
        name="diff_attention",
    )(lam.reshape(1).astype(F32), qkv, qkv, qkv, sub_g.reshape(1, LANES).astype(F32))


def _route(meta, cnt):
    bsz, _, nt = meta.shape
    n_pairs = bsz * nt * TOP_K
    tm = EXPERT_TILE
    n_rows = n_pairs + N_EXPERTS * tm
    n_tiles = n_rows // tm
    experts = jnp.arange(N_EXPERTS, dtype=jnp.int32)
    counts = cnt[0, :N_EXPERTS].astype(jnp.int32)
    padded = (counts + tm - 1) // tm * tm
    gend = jnp.cumsum(padded)
    gstart = gend - padded
    idx, rank = meta[:, :TOP_K, :], meta[:, TOP_K:, :]
    base = jnp.sum(jnp.where(idx[..., None] == experts, gstart, 0), axis=-1)
    pos = (base + rank).transpose(0, 2, 1).reshape(n_pairs).astype(jnp.int32)
    tile_start = jnp.arange(n_tiles, dtype=jnp.int32) * tm
    tile_expert = jnp.minimum(jnp.sum((gend[None, :] <= tile_start[:, None]).astype(jnp.int32), axis=1),
                              N_EXPERTS - 1)
    tile_rows = jnp.clip(counts[tile_expert] - (tile_start - gstart[tile_expert]), 0, tm).astype(jnp.int32)
    tile_first = (tile_start == gstart[tile_expert]).astype(jnp.int32)
    token = jnp.arange(n_pairs, dtype=jnp.int32) // TOP_K
    src = jnp.zeros((n_rows,), jnp.int32).at[pos].set(token, unique_indices=True)
    return pos, src.reshape(n_tiles, 1, tm), tile_expert.astype(jnp.int32), tile_rows, tile_first


def _gather_copy(h_hbm, buf, sem, slot):
    return pltpu.make_async_copy(h_hbm.at[pl.ds(0, buf.shape[1])], buf.at[slot], sem.at[slot])


def _ffn_kernel(te_ref, tr_ref, tf_ref, src0_ref, src1_ref, h_hbm, wgu_ref, bgu_ref, wd_ref, bd_ref, y_ref,
                buf, sem, wgu_bf, wd_bf):
    i = pl.program_id(0)
    n = pl.num_programs(0)
    tm = buf.shape[1]
    slot = i % 2

    def issue(src_ref, dst_slot):
        for r in range(tm):
            pltpu.make_async_copy(h_hbm.at[pl.ds(src_ref[0, 0, r], 1)],
                                  buf.at[dst_slot, pl.ds(r, 1)], sem.at[dst_slot]).start()

    @pl.when((i == 0) & (tr_ref[0] > 0))
    def _():
        issue(src0_ref, 0)

    valid = tr_ref[i] > 0
    prev_valid = (i > 0) & (tr_ref[jnp.maximum(i - 1, 0)] > 0)

    @pl.when(valid & (tf_ref[i] > 0))
    def _():
        wgu_bf[...] = wgu_ref[0].astype(BF16)
        wd_bf[...] = wd_ref[0].astype(BF16)

    for s in range(2):
        @pl.when(valid & (slot == s))
        def _():
            _gather_copy(h_hbm, buf, sem, s).wait()
            x = buf[s].astype(BF16)
            gu = jnp.dot(x, wgu_bf[...], preferred_element_type=F32) + bgu_ref[0]
            gate = jnp.minimum(gu[:, :D_EXPERT], SWIGLU_LIMIT)
            up = jnp.clip(gu[:, D_EXPERT:], -SWIGLU_LIMIT, SWIGLU_LIMIT)
            act = (up + 1.0) * (gate * jax.nn.sigmoid(SWIGLU_ALPHA * gate))
            y_ref[...] = jnp.dot(act.astype(BF16), wd_bf[...], preferred_element_type=F32) + bd_ref[0]
            issue(src1_ref, 1 - s)

    @pl.when(jnp.logical_not(valid))
    def _():
        y_ref[...] = jnp.zeros(y_ref.shape, y_ref.dtype)

    @pl.when(jnp.logical_not(valid) & prev_valid)
    def _():
        _gather_copy(h_hbm, buf, sem, slot).wait()


def _expert_ffn(h, src, tile_expert, tile_rows, tile_first, w_gu, b_gu, w_down, b_down):
    n_tok, d = h.shape
    n_tiles, _, tm = src.shape
    grid_spec = pltpu.PrefetchScalarGridSpec(
        num_scalar_prefetch=3,
        grid=(n_tiles,),
        in_specs=[
            pl.BlockSpec((1, 1, tm), lambda i, te, tr, tf: (i, 0, 0), memory_space=pltpu.SMEM),
            pl.BlockSpec((1, 1, tm), lambda i, te, tr, tf: (jnp.minimum(i + 1, n_tiles - 1), 0, 0),
                         memory_space=pltpu.SMEM),
            pl.BlockSpec(memory_space=pl.ANY),
            pl.BlockSpec((1, d, 2 * D_EXPERT), lambda i, te, tr, tf: (te[i], 0, 0)),
            pl.BlockSpec((1, 1, 2 * D_EXPERT), lambda i, te, tr, tf: (te[i], 0, 0)),
            pl.BlockSpec((1, D_EXPERT, d), lambda i, te, tr, tf: (te[i], 0, 0)),
            pl.BlockSpec((1, 1, d), lambda i, te, tr, tf: (te[i], 0, 0)),
        ],
        out_specs=pl.BlockSpec((tm, d), lambda i, te, tr, tf: (i, 0)),
        scratch_shapes=[pltpu.VMEM((2, tm, d), F32), pltpu.SemaphoreType.DMA((2,)),
                        pltpu.VMEM((d, 2 * D_EXPERT), BF16), pltpu.VMEM((D_EXPERT, d), BF16)],
    )
    return pl.pallas_call(
        _ffn_kernel,
        grid_spec=grid_spec,
        out_shape=jax.ShapeDtypeStruct((n_tiles * tm, d), F32),
        compiler_params=_params(("arbitrary",)),
        name="expert_ffn",
    )(tile_expert, tile_rows, tile_first, src, src, h, w_gu, b_gu.reshape(N_EXPERTS, 1, -1),
      w_down, b_down.reshape(N_EXPERTS, 1, -1))


def _combine_kernel(pos0_ref, pos1_ref, y_hbm, gate_ref, x_ref, m_ref, o_ref, buf, sem, *, i_gate):
    i = pl.program_id(0)
    n = pl.num_programs(0)
    tc = buf.shape[2]
    slot = i % 2

    def issue(pos_ref, dst_slot):
        for r in range(tc):
            for k in range(TOP_K):
                pltpu.make_async_copy(y_hbm.at[pl.ds(pos_ref[0, 0, r * TOP_K + k], 1)],
                                      buf.at[dst_slot, k, pl.ds(r, 1)], sem.at[dst_slot]).start()

    @pl.when(i == 0)
    def _():
        issue(pos0_ref, 0)

    def wait(s):
        for k in range(TOP_K):
            pltpu.make_async_copy(y_hbm.at[pl.ds(0, tc)], buf.at[s, k], sem.at[s]).wait()

    for s in range(2):
        @pl.when(slot == s)
        def _():
            wait(s)
            g = gate_ref[...]
            y = g[:, 0:1] * buf[s, 0]
            for k in range(1, TOP_K):
                y = y + g[:, k:k + 1] * buf[s, k]
            o_ref[...] = x_ref[...] + m_ref[0, 0, i_gate:i_gate + 1] * y
            issue(pos1_ref, 1 - s)

    @pl.when(i == n - 1)
    def _():
        wait(1 - slot)


def _combine(y, pos, gate, x, mods, i_gate, tiles_per_sample):
    n_tok, d = x.shape
    tc = ROW_TILE
    n_steps = n_tok // tc
    pos3 = pos.reshape(n_steps, 1, tc * TOP_K)

    def mod_map(i):
        return (i // tiles_per_sample, jnp.minimum(i % tiles_per_sample, 1), 0, 0)

    return pl.pallas_call(
        functools.partial(_combine_kernel, i_gate=i_gate),
        grid=(n_steps,),
        in_specs=[
            pl.BlockSpec((1, 1, tc * TOP_K), lambda i: (i, 0, 0), memory_space=pltpu.SMEM),
            pl.BlockSpec((1, 1, tc * TOP_K), lambda i: (jnp.minimum(i + 1, n_steps - 1), 0, 0),
                         memory_space=pltpu.SMEM),
            pl.BlockSpec(memory_space=pl.ANY),
            pl.BlockSpec((tc, LANES), lambda i: (i, 0)),
            pl.BlockSpec((tc, d), lambda i: (i, 0)),
            pl.BlockSpec((1, 1, 6, d), mod_map),
        ],
        out_specs=pl.BlockSpec((tc, d), lambda i: (i, 0)),
        out_shape=jax.ShapeDtypeStruct((n_tok, d), F32),
        scratch_shapes=[pltpu.VMEM((2, TOP_K, tc, d), F32), pltpu.SemaphoreType.DMA((2,))],
        compiler_params=_params(("arbitrary",)),
        name="moe_combine",
    )(pos3, pos3, y, gate, x, mods)


def _moe_block(x, g, mods, w_router, b_router, w_gu, b_gu, w_down, b_down):
    bsz, nt, d = x.shape
    h, gate, meta, cnt = _rms_router(x, g, mods, w_router, b_router, i_shift=3, i_scale=4)
    n_tok = bsz * nt
    pos, src, tile_expert, tile_rows, tile_first = _route(meta, cnt)
    y = _expert_ffn(h.reshape(n_tok, d), src, tile_expert, tile_rows, tile_first, w_gu, b_gu, w_down, b_down)
    out = _combine(y, pos, gate.reshape(n_tok, LANES), x.reshape(n_tok, d), mods, 5, nt // ROW_TILE)
    return out.reshape(bsz, nt, d)


def kernel(x, c, ctx, c_ctx, ada_w, ada_b, norm_attn_g, norm_ffn_g, na_w_qkv, na_rpb, na_w_o, sw_w_qkv, sw_sinks, sw_w_o, diff_w_qkv, diff_lam_q1, diff_lam_k1, diff_lam_q2, diff_lam_k2, diff_sub_g, diff_w_o, router_w, router_b, exp_w_gu, exp_b_gu, exp_w_down, exp_b_down, final_g):
    bsz, n_lat, d = x.shape
    n_ctx = ctx.shape[1]
    assert d == D_MODEL and n_ctx == CTX_LEN and n_lat % GRID_W == 0 and bsz < ADA_ROWS
    assert n_lat % ROW_TILE == 0

    cond = jnp.concatenate([c, c_ctx[None, :], jnp.zeros((ADA_ROWS - bsz - 1, d), c.dtype)], axis=0)
    ada = _ada_all(jax.nn.silu(cond).astype(BF16), ada_w, ada_b)
    rope = _rope_tables(n_ctx, n_lat)

    xs = jnp.concatenate([ctx, x], axis=1)
    for i in range(DEPTH):
        m_lat = ada[i, :bsz].reshape(bsz, 6, d)
        m_ctx = jnp.broadcast_to(ada[i, bsz].reshape(1, 6, d), (bsz, 6, d))
        mods = jnp.stack([m_ctx, m_lat], axis=1)
        h = _rms_mod(xs, norm_attn_g[i], mods, i_shift=0, i_scale=1)
        j = i // N_MIXERS
        kind = i % N_MIXERS
        if kind == 0:
            qkv = _qkv_proj(h, na_w_qkv[j], d, 0, None)
            o = _na_attention(qkv, _na_bias_table(na_rpb[j], n_lat // GRID_W), n_ctx)
            w_o = na_w_o[j]
        elif kind == 1:
            n_kv = N_KV_HEADS * HEAD_DIM
            qkv = _qkv_proj(h, sw_w_qkv[j], d, d + n_kv, rope)
            o = _sw_attention(qkv, sw_sinks[j], n_ctx)
            w_o = sw_w_o[j]
        else:
            lambda_init = 0.8 - 0.6 * math.exp(-0.3 * i)
            lam = (jnp.exp(jnp.sum(diff_lam_q1[j].astype(F32) * diff_lam_k1[j].astype(F32)))
                   - jnp.exp(jnp.sum(diff_lam_q2[j].astype(F32) * diff_lam_k2[j].astype(F32))) + lambda_init)
            qkv = _qkv_proj(h, diff_w_qkv[j], d, 2 * d, rope, q_scale=HEAD_DIM ** -0.5 * LOG2_E)
            o = _diff_attention(qkv, lam, diff_sub_g[j], lambda_init, n_ctx)
            w_o = diff_w_o[j]
        xs = _out_proj(o, w_o, xs, mods, n_ctx, i_gate=2)
        xs = _moe_block(xs, norm_ffn_g[i], mods, router_w[i], router_b[i], exp_w_gu[i], exp_b_gu[i],
                        exp_w_down[i], exp_b_down[i])
    return _final_norm(xs, final_g, n_ctx)
```

```python
import functools
import math

import jax
import jax.numpy as jnp
from jax import lax
from jax.experimental import pallas as pl
from jax.experimental.pallas import tpu as pltpu

D_MODEL = 2048
DEPTH = 4
CTX_LEN = 256
GRID_W = 64
N_MIXERS = 3
HEAD_DIM = 64
N_HEADS = D_MODEL // HEAD_DIM
N_KV_HEADS = N_HEADS // 8
GQA_GROUP = N_HEADS // N_KV_HEADS
N_DIFF_HEADS = N_HEADS // 2
NA_ROWS = 8
NA_COLS = 16
NA_QROWS = 4
NA_WIN = 12
SW_WINDOW = 128
Q_BLOCK = 128
DIFF_QBLOCK = 256
SW_STACK = 4
ROPE_BASE = 10000.0
N_EXPERTS = 32
TOP_K = 4
D_EXPERT = D_MODEL // 4
SWIGLU_LIMIT = 7.0
SWIGLU_ALPHA = 1.702
NORM_EPS = 1e-6
NEG_INF = -1e30
LOG2_E = 1.4426950408889634

LANES = 128
VMEM_LIMIT_BYTES = 52 * 1024 * 1024
ROW_TILE = 256
MM_ROW_TILES = (1088, 1024, 640, 512, 256)
EXPERT_TILE = 256
FFN_SLOTS = 3
COMBINE_SLOTS = 3
ADA_ROWS = 16

F32 = jnp.float32
BF16 = jnp.bfloat16


def _params(semantics):
    return pltpu.CompilerParams(dimension_semantics=semantics, vmem_limit_bytes=VMEM_LIMIT_BYTES)


def _mm_row_tile(nt):
    return next(t for t in MM_ROW_TILES if nt % t == 0)


def _dot_nt(a, b):
    return lax.dot_general(a, b, (((1,), (1,)), ((), ())), preferred_element_type=F32)


def _ada_kernel(s_ref, w_ref, b_ref, o_ref):
    o_ref[0] = jnp.dot(s_ref[...], w_ref[0].astype(BF16), preferred_element_type=F32) + b_ref[0]


def _ada_all(s, ada_w, ada_b):
    depth, d, n = ada_w.shape
    tn = 1024
    return pl.pallas_call(
        _ada_kernel,
        grid=(depth, n // tn),
        in_specs=[
            pl.BlockSpec((ADA_ROWS, d), lambda i, j: (0, 0)),
            pl.BlockSpec((1, d, tn), lambda i, j: (i, 0, j)),
            pl.BlockSpec((1, 1, tn), lambda i, j: (i, 0, j)),
        ],
        out_specs=pl.BlockSpec((1, ADA_ROWS, tn), lambda i, j: (i, 0, j)),
        out_shape=jax.ShapeDtypeStruct((depth, ADA_ROWS, n), F32),
        compiler_params=_params(("arbitrary", "arbitrary")),
        name="ada_ln",
    )(s, ada_w, ada_b.reshape(depth, 1, n))


def _normed(x, g, eps):
    ms = jnp.mean(x * x, axis=-1, keepdims=True)
    return x * lax.rsqrt(ms + eps) * g


def _rms_mod_kernel(x_ref, g_ref, m_ref, o_ref, *, i_shift, i_scale):
    m = m_ref[0, 0]
    y = _normed(x_ref[0], g_ref[...], NORM_EPS)
    o_ref[0] = (y * (1.0 + m[i_scale:i_scale + 1]) + m[i_shift:i_shift + 1]).astype(o_ref.dtype)


def _mods_spec():
    return pl.BlockSpec((1, 1, 6, D_MODEL), lambda b, j: (b, jnp.minimum(j, 1), 0, 0))


def _rms_mod(x, g, mods, i_shift, i_scale):
    bsz, nt, d = x.shape
    return pl.pallas_call(
        functools.partial(_rms_mod_kernel, i_shift=i_shift, i_scale=i_scale),
        grid=(bsz, nt // ROW_TILE),
        in_specs=[
            pl.BlockSpec((1, ROW_TILE, d), lambda b, j: (b, j, 0)),
            pl.BlockSpec((1, d), lambda b, j: (0, 0)),
            _mods_spec(),
        ],
        out_specs=pl.BlockSpec((1, ROW_TILE, d), lambda b, j: (b, j, 0)),
        out_shape=jax.ShapeDtypeStruct((bsz, nt, d), BF16),
        compiler_params=_params(("arbitrary", "arbitrary")),
        name="rms_mod",
    )(x, g.reshape(1, d), mods)


def _rms_router_kernel(x_ref, g_ref, m_ref, wr_ref, br_ref, h_ref, gate_ref, meta_ref, cnt_ref, *,
                       i_shift, i_scale):
    m = m_ref[0, 0]
    y = _normed(x_ref[0], g_ref[...], NORM_EPS)
    h = y * (1.0 + m[i_scale:i_scale + 1]) + m[i_shift:i_shift + 1]
    h_ref[0] = h
    h_hi = h.astype(BF16)
    h_lo = (h - h_hi.astype(F32)).astype(BF16)
    part = (jnp.dot(h_hi, wr_ref[...], preferred_element_type=F32)
            + jnp.dot(h_lo, wr_ref[...], preferred_element_type=F32))
    logits = part + pltpu.roll(part, LANES - N_EXPERTS, axis=1) + br_ref[...]
    lane = lax.broadcasted_iota(jnp.int32, logits.shape, 1)
    logits = jnp.where(lane < N_EXPERTS, logits, NEG_INF)
    vals, ids = [], []
    for _ in range(TOP_K):
        mx = jnp.max(logits, axis=-1, keepdims=True)
        ik = jnp.min(jnp.where(logits == mx, lane, LANES), axis=-1, keepdims=True)
        vals.append(mx)
        ids.append(ik)
        logits = jnp.where(lane == ik, -jnp.inf, logits)
    es = [jnp.exp(v - vals[0]) for v in vals]
    inv = 1.0 / (es[0] + es[1] + es[2] + es[3])
    gate = jnp.zeros(lane.shape, F32)
    for k in range(TOP_K):
        gate = jnp.where(lane == k, es[k] * inv, gate)
    gate_ref[0] = gate

    @pl.when((pl.program_id(0) == 0) & (pl.program_id(1) == 0))
    def _():
        cnt_ref[...] = jnp.zeros(cnt_ref.shape, cnt_ref.dtype)

    uses = jnp.zeros(lane.shape, F32)
    for k in range(TOP_K):
        uses = jnp.where(lane == ids[k], 1.0, uses)
    tr = uses.shape[0]
    tri = (lax.broadcasted_iota(jnp.int32, (tr, tr), 0) > lax.broadcasted_iota(jnp.int32, (tr, tr), 1))
    before = cnt_ref[...] + jnp.dot(jnp.where(tri, 1.0, 0.0).astype(BF16), uses.astype(BF16),
                                    preferred_element_type=F32)
    cnt_ref[...] += jnp.sum(uses, axis=0, keepdims=True)
    meta = jnp.zeros(lane.shape, jnp.int32)
    for k in range(TOP_K):
        rank = jnp.sum(jnp.where(lane == ids[k], before, 0.0), axis=-1, keepdims=True)
        meta = jnp.where(lane == k, ids[k], meta)
        meta = jnp.where(lane == TOP_K + k, rank.astype(jnp.int32), meta)
    meta_ref[0] = jnp.transpose(meta)[0:2 * TOP_K, :]


def _rms_router(x, g, mods, w_router, b_router, i_shift, i_scale):
    bsz, nt, d = x.shape
    w_hi = w_router.astype(BF16)
    w_lo = (w_router - w_hi.astype(F32)).astype(BF16)
    wr = jnp.zeros((d, LANES), BF16).at[:, :N_EXPERTS].set(w_hi).at[:, N_EXPERTS:2 * N_EXPERTS].set(w_lo)
    br = jnp.zeros((1, LANES), F32).at[0, :N_EXPERTS].set(b_router)
    row = pl.BlockSpec((1, ROW_TILE, d), lambda b, j: (b, j, 0))
    lanes = pl.BlockSpec((1, ROW_TILE, LANES), lambda b, j: (b, j, 0))
    return pl.pallas_call(
        functools.partial(_rms_router_kernel, i_shift=i_shift, i_scale=i_scale),
        grid=(bsz, nt // ROW_TILE),
        in_specs=[
            row,
            pl.BlockSpec((1, d), lambda b, j: (0, 0)),
            _mods_spec(),
            pl.BlockSpec((d, LANES), lambda b, j: (0, 0)),
            pl.BlockSpec((1, LANES), lambda b, j: (0, 0)),
        ],
        out_specs=[row, lanes,
                   pl.BlockSpec((1, 2 * TOP_K, ROW_TILE), lambda b, j: (b, 0, j)),
                   pl.BlockSpec((1, LANES), lambda b, j: (0, 0))],
        out_shape=[
            jax.ShapeDtypeStruct((bsz, nt, d), F32),
            jax.ShapeDtypeStruct((bsz, nt, LANES), F32),
            jax.ShapeDtypeStruct((bsz, 2 * TOP_K, nt), jnp.int32),
            jax.ShapeDtypeStruct((1, LANES), F32),
        ],
        compiler_params=_params(("arbitrary", "arbitrary")),
        name="rms_router",
    )(x, g.reshape(1, d), mods, wr, br)


def _final_norm_kernel(x_ref, g_ref, o_ref):
    o_ref[0] = _normed(x_ref[0], g_ref[...], NORM_EPS)


def _final_norm(x, g, n_ctx):
    bsz, nt, d = x.shape
    skip = n_ctx // ROW_TILE
    return pl.pallas_call(
        _final_norm_kernel,
        grid=(bsz, (nt - n_ctx) // ROW_TILE),
        in_specs=[
            pl.BlockSpec((1, ROW_TILE, d), lambda b, j: (b, j + skip, 0)),
            pl.BlockSpec((1, d), lambda b, j: (0, 0)),
        ],
        out_specs=pl.BlockSpec((1, ROW_TILE, d), lambda b, j: (b, j, 0)),
        out_shape=jax.ShapeDtypeStruct((bsz, nt - n_ctx, d), F32),
        compiler_params=_params(("arbitrary", "arbitrary")),
        name="final_norm",
    )(x, g.reshape(1, d))


def _rope_tables(n_ctx, n_lat):
    half = HEAD_DIM // 2
    nf = half // 2
    t = jnp.arange(n_lat)
    inv_freq = ROPE_BASE ** (-jnp.arange(nf, dtype=F32) / nf)
    lane = jnp.arange(LANES)
    d = lane % HEAD_DIM
    pos = jnp.where((d < half)[None, :], (t // GRID_W)[:, None], (t % GRID_W)[:, None]).astype(F32)
    ang = pos * inv_freq[d % nf][None, :]
    sign = jnp.where((d % half) < nf, -1.0, 1.0)[None, :]
    cos = jnp.concatenate([jnp.ones((n_ctx, LANES), F32), jnp.cos(ang)], axis=0)
    sin = jnp.concatenate([jnp.zeros((n_ctx, LANES), F32), jnp.sin(ang) * sign], axis=0)
    return cos, sin


def _qkv_kernel(x_ref, w_ref, *rest, n_q_chunks, n_rope_chunks, q_scale):
    if n_rope_chunks:
        cos_ref, sin_ref, o_ref = rest
    else:
        (o_ref,) = rest
    acc = jnp.dot(x_ref[0], w_ref[...], preferred_element_type=F32)
    chunks = acc.shape[1] // LANES
    j = pl.program_id(2)
    nf = HEAD_DIM // 4
    for c in range(chunks):
        gc = j * chunks + c
        a = acc[:, c * LANES:(c + 1) * LANES]
        if n_rope_chunks:
            lane = lax.broadcasted_iota(jnp.int32, a.shape, 1)
            swapped = jnp.where((lane % (2 * nf)) < nf,
                                pltpu.roll(a, LANES - nf, axis=1), pltpu.roll(a, nf, axis=1))
            use = gc < n_rope_chunks
            a = a * jnp.where(use, cos_ref[...], 1.0) + swapped * jnp.where(use, sin_ref[...], 0.0)
        a = a * jnp.where(gc < n_q_chunks, q_scale, 1.0)
        o_ref[0, :, c * LANES:(c + 1) * LANES] = a.astype(o_ref.dtype)


def _qkv_proj(h, w, n_q_cols, n_rope_cols, rope, q_scale=HEAD_DIM ** -0.5):
    bsz, nt, d = h.shape
    n = w.shape[1]
    tn = 512
    tm = _mm_row_tile(nt)
    in_specs = [
        pl.BlockSpec((1, tm, d), lambda b, i, j: (b, i, 0)),
        pl.BlockSpec((d, tn), lambda b, i, j: (0, j)),
    ]
    args = [h, w.astype(BF16)]
    if n_rope_cols:
        in_specs += [pl.BlockSpec((tm, LANES), lambda b, i, j: (i, 0))] * 2
        args += list(rope)
    return pl.pallas_call(
        functools.partial(_qkv_kernel, n_q_chunks=n_q_cols // LANES, n_rope_chunks=n_rope_cols // LANES,
                          q_scale=q_scale),
        grid=(bsz, nt // tm, n // tn),
        in_specs=in_specs,
        out_specs=pl.BlockSpec((1, tm, tn), lambda b, i, j: (b, i, j)),
        out_shape=jax.ShapeDtypeStruct((bsz, nt, n), BF16),
        compiler_params=_params(("arbitrary", "arbitrary", "arbitrary")),
        name="qkv_proj",
    )(*args)


def _out_proj_kernel(o_ref, w_ref, x_ref, m_ref, y_ref, *, n_ctx, i_gate):
    acc = jnp.dot(o_ref[0], w_ref[...], preferred_element_type=F32)
    tm = acc.shape[0]
    row = pl.program_id(1) * tm + lax.broadcasted_iota(jnp.int32, (tm, 1), 0)
    m = m_ref[0]
    gate = jnp.where(row < n_ctx, m[0, i_gate:i_gate + 1], m[1, i_gate:i_gate + 1])
    y_ref[0] = x_ref[0] + gate * acc


def _out_proj(o, w, x, mods, n_ctx, i_gate):
    bsz, nt, d = o.shape
    n = w.shape[1]
    tn = 512
    tm = _mm_row_tile(nt)
    return pl.pallas_call(
        functools.partial(_out_proj_kernel, n_ctx=n_ctx, i_gate=i_gate),
        grid=(bsz, nt // tm, n // tn),
        in_specs=[
            pl.BlockSpec((1, tm, d), lambda b, i, j: (b, i, 0)),
            pl.BlockSpec((d, tn), lambda b, i, j: (0, j)),
            pl.BlockSpec((1, tm, tn), lambda b, i, j: (b, i, j)),
            pl.BlockSpec((1, 2, 6, tn), lambda b, i, j: (b, 0, 0, j)),
        ],
        out_specs=pl.BlockSpec((1, tm, tn), lambda b, i, j: (b, i, j)),
        out_shape=jax.ShapeDtypeStruct((bsz, nt, n), F32),
        compiler_params=_params(("arbitrary", "arbitrary", "arbitrary")),
        name="out_proj",
    )(o, w.astype(BF16), x, mods)


def _softmax_pv(s, v, extra=None, base2=False):
    exp = jnp.exp2 if base2 else jnp.exp
    mx = jnp.max(s, axis=-1, keepdims=True)
    if extra is not None:
        mx = jnp.maximum(mx, extra)
    e = exp(s - mx)
    den = jnp.sum(e, axis=-1, keepdims=True)
    if extra is not None:
        den = den + exp(extra - mx)
    return jnp.dot(e.astype(BF16), v, preferred_element_type=F32) / den


def _split_heads(x):
    lo = lax.broadcasted_iota(jnp.int32, x.shape, 1) < HEAD_DIM
    zero = jnp.zeros_like(x)
    return jnp.concatenate([jnp.where(lo, x, zero), jnp.where(lo, zero, x)], axis=0)


def _merge_heads(o):
    n = o.shape[0] // 2
    lo = lax.broadcasted_iota(jnp.int32, (n, o.shape[1]), 1) < HEAD_DIM
    return jnp.where(lo, o[:n], o[n:])


def _na_block_geometry(rows):
    n_blk = rows // NA_QROWS
    r0 = [j * NA_QROWS for j in range(n_blk)]
    ks = [min(max(r - NA_ROWS // 2, 0), rows - NA_WIN) for r in r0]
    cfg = [0 if j == 0 else (2 if j == n_blk - 1 else 1) for j in range(n_blk)]
    return r0, ks, cfg


def _na_bias_table(rpb):
    qc = jnp.arange(GRID_W)[:, None]
    kc = jnp.arange(GRID_W)[None, :]
    win = jnp.clip(qc - NA_COLS // 2, 0, GRID_W - NA_COLS)
    col_ok = (kc >= win) & (kc < win + NA_COLS)
    col_off = jnp.clip(kc - qc + NA_COLS - 1, 0, 2 * NA_COLS - 2)
    pick = (col_off[None] == jnp.arange(2 * NA_COLS - 1)[:, None, None]).astype(F32)
    tab = jnp.einsum("hrd,dqk->hrqk", rpb.astype(F32), pick, precision=lax.Precision.HIGHEST)
    return jnp.where(col_ok[None, None], tab, NEG_INF)


def _na_kernel(q_ref, k_ref, v_ref, bias_ref, o_ref, tab, *, n_ctx, rows):
    nq = NA_QROWS * GRID_W
    nwin = NA_WIN * GRID_W
    n_blk = rows // NA_QROWS

    @pl.when(pl.program_id(1) == 0)
    def _():
        r0s, kss, cfgs = _na_block_geometry(rows)
        assert all(r0s[j] - kss[j] == r0s[1] - kss[1] for j in range(n_blk) if cfgs[j] == 1)
        tab[...] = jnp.full(tab.shape, NEG_INF, tab.dtype)
        for c in range(3):
            j = cfgs.index(c)
            for a in range(NA_QROWS):
                r = r0s[j] + a
                rs = min(max(r - NA_ROWS // 2, 0), rows - NA_ROWS)
                for i in range(NA_WIN):
                    kr = kss[j] + i
                    if rs <= kr < rs + NA_ROWS:
                        for hh in range(2):
                            tab[c, hh * nq + a * GRID_W:hh * nq + (a + 1) * GRID_W, i * GRID_W:(i + 1) * GRID_W] = (
                                bias_ref[hh, kr - r + NA_ROWS - 1])

    s = _dot_nt(_split_heads(q_ref[0, 0:n_ctx, :]), k_ref[0, 0:n_ctx, :])
    o_ref[0, 0:n_ctx, :] = _merge_heads(_softmax_pv(s, v_ref[0, 0:n_ctx, :])).astype(o_ref.dtype)

    def block(j, carry):
        r0 = j * NA_QROWS
        ks = jnp.clip(r0 - NA_ROWS // 2, 0, rows - NA_WIN)
        cfg = jnp.where(j == 0, 0, jnp.where(j == n_blk - 1, 2, 1))
        q0 = pl.multiple_of(n_ctx + r0 * GRID_W, GRID_W)
        k0 = pl.multiple_of(n_ctx + ks * GRID_W, GRID_W)
        kk = jnp.concatenate([k_ref[0, 0:n_ctx, :], k_ref[0, pl.ds(k0, nwin), :]], axis=0)
        vv = jnp.concatenate([v_ref[0, 0:n_ctx, :], v_ref[0, pl.ds(k0, nwin), :]], axis=0)
        s = _dot_nt(_split_heads(q_ref[0, pl.ds(q0, nq), :]), kk)
        s = jnp.concatenate([s[:, :n_ctx], s[:, n_ctx:] + tab[cfg]], axis=1)
        o_ref[0, pl.ds(q0, nq), :] = _merge_heads(_softmax_pv(s, vv)).astype(o_ref.dtype)
        return carry

    lax.fori_loop(0, n_blk, block, 0, unroll=4)


def _na_attention(qkv, bias_tab, n_ctx):
    bsz, nt, _ = qkv.shape
    rows = (nt - n_ctx) // GRID_W
    n_hp = N_HEADS // 2
    blk = lambda off: pl.BlockSpec((1, nt, LANES), lambda p, b: (b, 0, p + off))
    return pl.pallas_call(
        functools.partial(_na_kernel, n_ctx=n_ctx, rows=rows),
        grid=(n_hp, bsz),
        in_specs=[
            blk(0), blk(n_hp), blk(2 * n_hp),
            pl.BlockSpec((2,) + bias_tab.shape[1:], lambda p, b: (p, 0, 0, 0)),
        ],
        out_specs=pl.BlockSpec((1, nt, LANES), lambda p, b: (b, 0, p)),
        out_shape=jax.ShapeDtypeStruct((bsz, nt, D_MODEL), BF16),
        scratch_shapes=[pltpu.VMEM((3, 2 * NA_QROWS * GRID_W, NA_WIN * GRID_W), F32)],
        compiler_params=_params(("arbitrary", "arbitrary")),
        name="na_attention",
    )(qkv, qkv, qkv, bias_tab)


def _sw_kernel(sink_ref, q_ref, k_ref, v_ref, o_ref, *, n_ctx, n_lat):
    hd = HEAD_DIM
    qb = Q_BLOCK
    n_blk = n_lat // qb
    p = pl.program_id(1)
    jl = pl.program_id(2) - n_ctx // qb
    starts, valids = [], []
    for c in range(3):
        blk = jl - 1 + c
        starts.append(pl.multiple_of(n_ctx + qb * jnp.clip(blk, 0, n_blk - 1), qb))
        valids.append((blk >= 0) & (blk < n_blk) & (jl >= 0))
    iq = lax.broadcasted_iota(jnp.int32, (qb, qb), 0)
    ik = lax.broadcasted_iota(jnp.int32, (qb, qb), 1)
    oks = [valids[c] & (jnp.abs(iq - ik - (c - 1) * qb) <= SW_WINDOW) for c in range(3)]
    ok = jnp.concatenate(oks + [jnp.full((qb, n_ctx), True)], axis=1)
    k_all = jnp.concatenate([k_ref[0, pl.ds(s0, qb), :] for s0 in starts] + [k_ref[0, 0:n_ctx, :]], axis=0)
    v_all = jnp.concatenate([v_ref[0, pl.ds(s0, qb), :] for s0 in starts] + [v_ref[0, 0:n_ctx, :]], axis=0)
    gw = GQA_GROUP * hd
    for g in range(2):
        kg = jnp.concatenate([k_all[:, g * hd:(g + 1) * hd]] * 2, axis=1)
        vg = jnp.concatenate([v_all[:, g * hd:(g + 1) * hd]] * 2, axis=1)
        for c0 in range(0, GQA_GROUP // 2, SW_STACK // 2):
            cols = [g * gw + c * LANES for c in range(c0, c0 + SW_STACK // 2)]
            qs = jnp.concatenate([_split_heads(q_ref[0, :, c:c + LANES]) for c in cols], axis=0)
            s = _dot_nt(qs, kg)
            s = jnp.concatenate([jnp.where(ok, s[i * qb:(i + 1) * qb], NEG_INF) for i in range(SW_STACK)], axis=0)
            sink = jnp.concatenate([jnp.full((qb, 1), sink_ref[(p * 2 + g) * GQA_GROUP + 2 * c0 + i], F32)
                                    for i in range(SW_STACK)], axis=0)
            o = _softmax_pv(s, vg, extra=sink)
            for n, c in enumerate(cols):
                o_ref[0, :, c:c + LANES] = _merge_heads(o[2 * n * qb:(2 * n + 2) * qb]).astype(o_ref.dtype)


def _sw_attention(qkv, sinks, n_ctx):
    bsz, nt, _ = qkv.shape
    n_lat = nt - n_ctx
    qw = 2 * GQA_GROUP * HEAD_DIM
    k_blk = D_MODEL // LANES
    v_blk = k_blk + N_KV_HEADS * HEAD_DIM // LANES
    grid_spec = pltpu.PrefetchScalarGridSpec(
        num_scalar_prefetch=1,
        grid=(bsz, N_KV_HEADS // 2, nt // Q_BLOCK),
        in_specs=[
            pl.BlockSpec((1, Q_BLOCK, qw), lambda b, p, j, s: (b, j, p)),
            pl.BlockSpec((1, nt, LANES), lambda b, p, j, s: (b, 0, k_blk + p)),
            pl.BlockSpec((1, nt, LANES), lambda b, p, j, s: (b, 0, v_blk + p)),
        ],
        out_specs=pl.BlockSpec((1, Q_BLOCK, qw), lambda b, p, j, s: (b, j, p)),
    )
    return pl.pallas_call(
        functools.partial(_sw_kernel, n_ctx=n_ctx, n_lat=n_lat),
        grid_spec=grid_spec,
        out_shape=jax.ShapeDtypeStruct((bsz, nt, D_MODEL), BF16),
        compiler_params=_params(("arbitrary", "arbitrary", "arbitrary")),
        name="sw_attention",
    )(sinks.astype(F32), qkv, qkv, qkv)


def _diff_kernel(lam_ref, q_ref, k_ref, v_ref, g_ref, o_ref, *, n_ctx, lambda_init):
    lam = lam_ref[0]
    nt = k_ref.shape[1]
    tq = DIFF_QBLOCK

    def attend(q, n_keys):
        n = q.shape[0]
        qs = _split_heads(q)
        k = k_ref[0, 0:n_keys, :]
        v = v_ref[0, 0:n_keys, :]
        outs = [_softmax_pv(_dot_nt(qs[i * n:(i + 1) * n], k), v, base2=True) for i in range(2)]
        o = outs[0] - lam * outs[1]
        return (_normed(o, g_ref[...], 1e-5) * (1.0 - lambda_init)).astype(o_ref.dtype)

    o_ref[0, 0:n_ctx, :] = attend(q_ref[0, 0:n_ctx, :], n_ctx)

    def block(j, carry):
        r0 = pl.multiple_of(n_ctx + j * tq, LANES)
        o_ref[0, pl.ds(r0, tq), :] = attend(q_ref[0, pl.ds(r0, tq), :], nt)
        return carry

    lax.fori_loop(0, (nt - n_ctx) // tq, block, 0, unroll=4)


def _diff_attention(qkv, lam, sub_g, lambda_init, n_ctx):
    bsz, nt, _ = qkv.shape
    nh = N_DIFF_HEADS
    assert (nt - n_ctx) % DIFF_QBLOCK == 0
    grid_spec = pltpu.PrefetchScalarGridSpec(
        num_scalar_prefetch=1,
        grid=(bsz, nh),
        in_specs=[
            pl.BlockSpec((1, nt, LANES), lambda b, h, s: (b, 0, h)),
            pl.BlockSpec((1, nt, LANES), lambda b, h, s: (b, 0, nh + h)),
            pl.BlockSpec((1, nt, LANES), lambda b, h, s: (b, 0, 2 * nh + h)),
            pl.BlockSpec((1, LANES), lambda b, h, s: (0, 0)),
        ],
        out_specs=pl.BlockSpec((1, nt, LANES), lambda b, h, s: (b, 0, h)),
    )
    return pl.pallas_call(
        functools.partial(_diff_kernel, n_ctx=n_ctx, lambda_init=lambda_init),
        grid_spec=grid_spec,
        out_shape=jax.ShapeDtypeStruct((bsz, nt, D_MODEL), BF16),
        compiler_params=_params(("arbitrary", "arbitrary")),
        name="diff_attention",
    )(lam.reshape(1).astype(F32), qkv, qkv, qkv, sub_g.reshape(1, LANES).astype(F32))


def _route(meta, cnt):
    bsz, _, nt = meta.shape
    n_pairs = bsz * nt * TOP_K
    tm = EXPERT_TILE
    n_rows = n_pairs + (N_EXPERTS + FFN_SLOTS - 1) * tm
    n_tiles = n_rows // tm
    experts = jnp.arange(N_EXPERTS, dtype=jnp.int32)
    counts = cnt[0, :N_EXPERTS].astype(jnp.int32)
    padded = (counts + tm - 1) // tm * tm
    gend = jnp.cumsum(padded)
    gstart = gend - padded
    idx, rank = meta[:, :TOP_K, :], meta[:, TOP_K:, :]
    base = jnp.sum(jnp.where(idx[..., None] == experts, gstart, 0), axis=-1)
    pos = (base + rank).transpose(0, 2, 1).reshape(n_pairs).astype(jnp.int32)
    tile_start = jnp.arange(n_tiles, dtype=jnp.int32) * tm
    tile_expert = jnp.minimum(jnp.sum((gend[None, :] <= tile_start[:, None]).astype(jnp.int32), axis=1),
                              N_EXPERTS - 1)
    tile_rows = jnp.clip(counts[tile_expert] - (tile_start - gstart[tile_expert]), 0, tm).astype(jnp.int32)
    tile_first = (tile_start == gstart[tile_expert]).astype(jnp.int32)
    token = jnp.arange(n_pairs, dtype=jnp.int32) // TOP_K
    src = jnp.zeros((n_rows,), jnp.int32).at[pos].set(token, unique_indices=True)
    return pos, src.reshape(n_tiles, 1, tm), tile_expert.astype(jnp.int32), tile_rows, tile_first


def _gather_copy(h_hbm, buf, sem, slot):
    return pltpu.make_async_copy(h_hbm.at[pl.ds(0, buf.shape[1])], buf.at[slot], sem.at[slot])


def _ffn_kernel(te_ref, tr_ref, tf_ref, src0_ref, src1_ref, src2_ref, h_hbm, wgu_ref, bgu_ref, wd_ref, bd_ref,
                y_ref, buf, sem, wgu_bf, wd_bf):
    i = pl.program_id(0)
    tm = buf.shape[1]
    slot = i % FFN_SLOTS

    def issue(src_ref, dst_slot):
        for r in range(tm):
            pltpu.make_async_copy(h_hbm.at[pl.ds(src_ref[0, 0, r], 1)],
                                  buf.at[dst_slot, pl.ds(r, 1)], sem.at[dst_slot]).start()

    @pl.when(i == 0)
    def _():
        issue(src0_ref, 0)
        issue(src1_ref, 1)

    valid = tr_ref[i] > 0
    fetched = (i < 2) | (tr_ref[jnp.maximum(i - 2, 0)] > 0)

    @pl.when(valid & (tf_ref[i] > 0))
    def _():
        wgu_bf[...] = wgu_ref[0].astype(BF16)
        wd_bf[...] = wd_ref[0].astype(BF16)

    for s in range(FFN_SLOTS):
        @pl.when(valid & (slot == s))
        def _():
            _gather_copy(h_hbm, buf, sem, s).wait()
            x = buf[s].astype(BF16)
            gu = jnp.dot(x, wgu_bf[...], preferred_element_type=F32) + bgu_ref[0]
            gate = jnp.minimum(gu[:, :D_EXPERT], SWIGLU_LIMIT)
            up = jnp.clip(gu[:, D_EXPERT:], -SWIGLU_LIMIT, SWIGLU_LIMIT)
            act = (up + 1.0) * (gate * jax.nn.sigmoid(SWIGLU_ALPHA * gate))
            y_ref[...] = jnp.dot(act.astype(BF16), wd_bf[...], preferred_element_type=F32) + bd_ref[0]
            issue(src2_ref, (s + 2) % FFN_SLOTS)

    @pl.when(jnp.logical_not(valid))
    def _():
        y_ref[...] = jnp.zeros(y_ref.shape, y_ref.dtype)

    @pl.when(jnp.logical_not(valid) & fetched)
    def _():
        _gather_copy(h_hbm, buf, sem, slot).wait()


def _expert_ffn(h, src, tile_expert, tile_rows, tile_first, layer, w_gu, b_gu, w_down, b_down):
    n_tok, d = h.shape
    n_tiles, _, tm = src.shape
    n_all = w_gu.shape[0] * N_EXPERTS

    def src_spec(ahead):
        return pl.BlockSpec((1, 1, tm), lambda i, te, tr, tf: (jnp.minimum(i + ahead, n_tiles - 1), 0, 0),
                            memory_space=pltpu.SMEM)

    def w_spec(shape):
        return pl.BlockSpec((1,) + shape, lambda i, te, tr, tf: (layer * N_EXPERTS + te[i], 0, 0))

    grid_spec = pltpu.PrefetchScalarGridSpec(
        num_scalar_prefetch=3,
        grid=(n_tiles,),
        in_specs=[
            src_spec(0), src_spec(1), src_spec(2),
            pl.BlockSpec(memory_space=pl.ANY),
            w_spec((d, 2 * D_EXPERT)), w_spec((1, 2 * D_EXPERT)), w_spec((D_EXPERT, d)), w_spec((1, d)),
        ],
        out_specs=pl.BlockSpec((tm, d), lambda i, te, tr, tf: (i, 0)),
        scratch_shapes=[pltpu.VMEM((FFN_SLOTS, tm, d), F32), pltpu.SemaphoreType.DMA((FFN_SLOTS,)),
                        pltpu.VMEM((d, 2 * D_EXPERT), BF16), pltpu.VMEM((D_EXPERT, d), BF16)],
    )
    return pl.pallas_call(
        _ffn_kernel,
        grid_spec=grid_spec,
        out_shape=jax.ShapeDtypeStruct((n_tiles * tm, d), F32),
        compiler_params=_params(("arbitrary",)),
        name="expert_ffn",
    )(tile_expert, tile_rows, tile_first, src, src, src, h,
      w_gu.reshape(n_all, d, 2 * D_EXPERT), b_gu.reshape(n_all, 1, 2 * D_EXPERT),
      w_down.reshape(n_all, D_EXPERT, d), b_down.reshape(n_all, 1, d))


def _combine_kernel(pos0_ref, pos1_ref, pos2_ref, y_hbm, gate_ref, x_ref, m_ref, o_ref, buf, sem, *, i_gate):
    i = pl.program_id(0)
    n = pl.num_programs(0)
    tc = buf.shape[2]
    slot = i % COMBINE_SLOTS

    def issue(pos_ref, dst_slot):
        for r in range(tc):
            for k in range(TOP_K):
                pltpu.make_async_copy(y_hbm.at[pl.ds(pos_ref[0, 0, r * TOP_K + k], 1)],
                                      buf.at[dst_slot, k, pl.ds(r, 1)], sem.at[dst_slot]).start()

    @pl.when(i == 0)
    def _():
        issue(pos0_ref, 0)
        issue(pos1_ref, 1)

    def wait(s):
        for k in range(TOP_K):
            pltpu.make_async_copy(y_hbm.at[pl.ds(0, tc)], buf.at[s, k], sem.at[s]).wait()

    for s in range(COMBINE_SLOTS):
        @pl.when(slot == s)
        def _():
            wait(s)
            g = gate_ref[...]
            y = g[:, 0:1] * buf[s, 0]
            for k in range(1, TOP_K):
                y = y + g[:, k:k + 1] * buf[s, k]
            o_ref[...] = x_ref[...] + m_ref[0, 0, i_gate:i_gate + 1] * y
            issue(pos2_ref, (s + 2) % COMBINE_SLOTS)

    @pl.when(i == n - 1)
    def _():
        wait((slot + 1) % COMBINE_SLOTS)
        wait((slot + 2) % COMBINE_SLOTS)


def _combine(y, pos, gate, x, mods, i_gate, tiles_per_sample):
    n_tok, d = x.shape
    tc = ROW_TILE
    n_steps = n_tok // tc
    pos3 = pos.reshape(n_steps, 1, tc * TOP_K)

    assert n_steps >= 2

    def mod_map(i):
        return (i // tiles_per_sample, jnp.minimum(i % tiles_per_sample, 1), 0, 0)

    def pos_spec(ahead):
        return pl.BlockSpec((1, 1, tc * TOP_K), lambda i: (jnp.minimum(i + ahead, n_steps - 1), 0, 0),
                            memory_space=pltpu.SMEM)

    return pl.pallas_call(
        functools.partial(_combine_kernel, i_gate=i_gate),
        grid=(n_steps,),
        in_specs=[
            pos_spec(0), pos_spec(1), pos_spec(2),
            pl.BlockSpec(memory_space=pl.ANY),
            pl.BlockSpec((tc, LANES), lambda i: (i, 0)),
            pl.BlockSpec((tc, d), lambda i: (i, 0)),
            pl.BlockSpec((1, 1, 6, d), mod_map),
        ],
        out_specs=pl.BlockSpec((tc, d), lambda i: (i, 0)),
        out_shape=jax.ShapeDtypeStruct((n_tok, d), F32),
        scratch_shapes=[pltpu.VMEM((COMBINE_SLOTS, TOP_K, tc, d), F32), pltpu.SemaphoreType.DMA((COMBINE_SLOTS,))],
        compiler_params=_params(("arbitrary",)),
        name="moe_combine",
    )(pos3, pos3, pos3, y, gate, x, mods)


def _moe_block(x, g, mods, w_router, b_router, layer, w_gu, b_gu, w_down, b_down):
    bsz, nt, d = x.shape
    h, gate, meta, cnt = _rms_router(x, g, mods, w_router, b_router, i_shift=3, i_scale=4)
    n_tok = bsz * nt
    pos, src, tile_expert, tile_rows, tile_first = _route(meta, cnt)
    y = _expert_ffn(h.reshape(n_tok, d), src, tile_expert, tile_rows, tile_first, layer,
                    w_gu, b_gu, w_down, b_down)
    out = _combine(y, pos, gate.reshape(n_tok, LANES), x.reshape(n_tok, d), mods, 5, nt // ROW_TILE)
    return out.reshape(bsz, nt, d)


def kernel(x, c, ctx, c_ctx, ada_w, ada_b, norm_attn_g, norm_ffn_g, na_w_qkv, na_rpb, na_w_o, sw_w_qkv, sw_sinks, sw_w_o, diff_w_qkv, diff_lam_q1, diff_lam_k1, diff_lam_q2, diff_lam_k2, diff_sub_g, diff_w_o, router_w, router_b, exp_w_gu, exp_b_gu, exp_w_down, exp_b_down, final_g):
    bsz, n_lat, d = x.shape
    n_ctx = ctx.shape[1]
    assert d == D_MODEL and n_ctx == CTX_LEN and n_lat % GRID_W == 0 and bsz < ADA_ROWS
    assert n_lat % ROW_TILE == 0

    cond = jnp.concatenate([c, c_ctx[None, :], jnp.zeros((ADA_ROWS - bsz - 1, d), c.dtype)], axis=0)
    ada = _ada_all(jax.nn.silu(cond).astype(BF16), ada_w, ada_b)
    rope = _rope_tables(n_ctx, n_lat)

    xs = jnp.concatenate([ctx, x], axis=1)
    for i in range(DEPTH):
        m_lat = ada[i, :bsz].reshape(bsz, 6, d)
        m_ctx = jnp.broadcast_to(ada[i, bsz].reshape(1, 6, d), (bsz, 6, d))
        mods = jnp.stack([m_ctx, m_lat], axis=1)
        h = _rms_mod(xs, norm_attn_g[i], mods, i_shift=0, i_scale=1)
        j = i // N_MIXERS
        kind = i % N_MIXERS
        if kind == 0:
            qkv = _qkv_proj(h, na_w_qkv[j], d, 0, None)
            o = _na_attention(qkv, _na_bias_table(na_rpb[j]), n_ctx)
            w_o = na_w_o[j]
        elif kind == 1:
            n_kv = N_KV_HEADS * HEAD_DIM
            qkv = _qkv_proj(h, sw_w_qkv[j], d, d + n_kv, rope)
            o = _sw_attention(qkv, sw_sinks[j], n_ctx)
            w_o = sw_w_o[j]
        else:
            lambda_init = 0.8 - 0.6 * math.exp(-0.3 * i)
            lam = (jnp.exp(jnp.sum(diff_lam_q1[j].astype(F32) * diff_lam_k1[j].astype(F32)))
                   - jnp.exp(jnp.sum(diff_lam_q2[j].astype(F32) * diff_lam_k2[j].astype(F32))) + lambda_init)
            qkv = _qkv_proj(h, diff_w_qkv[j], d, 2 * d, rope, q_scale=HEAD_DIM ** -0.5 * LOG2_E)
            o = _diff_attention(qkv, lam, diff_sub_g[j], lambda_init, n_ctx)
            w_o = diff_w_o[j]
        xs = _out_proj(o, w_o, xs, mods, n_ctx, i_gate=2)
        xs = _moe_block(xs, norm_ffn_g[i], mods, router_w[i], router_b[i], i, exp_w_gu, exp_b_gu,
                        exp_w_down, exp_b_down)
    return _final_norm(xs, final_g, n_ctx)
```

```python
import functools
import math

import jax
import jax.numpy as jnp
from jax import lax
from jax.experimental import pallas as pl
from jax.experimental.pallas import tpu as pltpu

D_MODEL = 2048
DEPTH = 4
CTX_LEN = 256
GRID_W = 64
N_MIXERS = 3
HEAD_DIM = 64
N_HEADS = D_MODEL // HEAD_DIM
N_KV_HEADS = N_HEADS // 8
GQA_GROUP = N_HEADS // N_KV_HEADS
N_DIFF_HEADS = N_HEADS // 2
NA_ROWS = 8
NA_COLS = 16
NA_QROWS = 4
NA_WIN = 12
SW_WINDOW = 128
Q_BLOCK = 128
DIFF_QBLOCK = 256
SW_STACK = 4
ROPE_BASE = 10000.0
N_EXPERTS = 32
TOP_K = 4
D_EXPERT = D_MODEL // 4
SWIGLU_LIMIT = 7.0
SWIGLU_ALPHA = 1.702
NORM_EPS = 1e-6
NEG_INF = -1e30
LOG2_E = 1.4426950408889634

LANES = 128
VMEM_LIMIT_BYTES = 52 * 1024 * 1024
ROW_TILE = 256
MM_ROW_TILES = (1088, 1024, 640, 512, 256)
EXPERT_TILE = 256
FFN_SLOTS = 3
COMBINE_SLOTS = 3
ADA_ROWS = 16

F32 = jnp.float32
BF16 = jnp.bfloat16


def _params(semantics):
    return pltpu.CompilerParams(dimension_semantics=semantics, vmem_limit_bytes=VMEM_LIMIT_BYTES)


def _mm_row_tile(nt):
    return next(t for t in MM_ROW_TILES if nt % t == 0)


def _dot_nt(a, b):
    return lax.dot_general(a, b, (((1,), (1,)), ((), ())), preferred_element_type=F32)


def _ada_kernel(s_ref, w_ref, b_ref, o_ref):
    o_ref[0] = jnp.dot(s_ref[...], w_ref[0].astype(BF16), preferred_element_type=F32) + b_ref[0]


def _ada_all(s, ada_w, ada_b):
    depth, d, n = ada_w.shape
    tn = 1024
    return pl.pallas_call(
        _ada_kernel,
        grid=(depth, n // tn),
        in_specs=[
            pl.BlockSpec((ADA_ROWS, d), lambda i, j: (0, 0)),
            pl.BlockSpec((1, d, tn), lambda i, j: (i, 0, j)),
            pl.BlockSpec((1, 1, tn), lambda i, j: (i, 0, j)),
        ],
        out_specs=pl.BlockSpec((1, ADA_ROWS, tn), lambda i, j: (i, 0, j)),
        out_shape=jax.ShapeDtypeStruct((depth, ADA_ROWS, n), F32),
        compiler_params=_params(("arbitrary", "arbitrary")),
        name="ada_ln",
    )(s, ada_w, ada_b.reshape(depth, 1, n))


def _normed(x, g, eps):
    ms = jnp.mean(x * x, axis=-1, keepdims=True)
    return x * lax.rsqrt(ms + eps) * g


def _rms_mod_kernel(x_ref, g_ref, m_ref, o_ref, *, i_shift, i_scale):
    m = m_ref[0, 0]
    y = _normed(x_ref[0], g_ref[...], NORM_EPS)
    o_ref[0] = (y * (1.0 + m[i_scale:i_scale + 1]) + m[i_shift:i_shift + 1]).astype(o_ref.dtype)


def _mods_spec():
    return pl.BlockSpec((1, 1, 6, D_MODEL), lambda b, j: (b, jnp.minimum(j, 1), 0, 0))


def _rms_mod(x, g, mods, i_shift, i_scale):
    bsz, nt, d = x.shape
    return pl.pallas_call(
        functools.partial(_rms_mod_kernel, i_shift=i_shift, i_scale=i_scale),
        grid=(bsz, nt // ROW_TILE),
        in_specs=[
            pl.BlockSpec((1, ROW_TILE, d), lambda b, j: (b, j, 0)),
            pl.BlockSpec((1, d), lambda b, j: (0, 0)),
            _mods_spec(),
        ],
        out_specs=pl.BlockSpec((1, ROW_TILE, d), lambda b, j: (b, j, 0)),
        out_shape=jax.ShapeDtypeStruct((bsz, nt, d), BF16),
        compiler_params=_params(("arbitrary", "arbitrary")),
        name="rms_mod",
    )(x, g.reshape(1, d), mods)


def _rms_router_kernel(x_ref, g_ref, m_ref, wr_ref, br_ref, h_ref, gate_ref, meta_ref, cnt_ref, grid_ref, base_ref, *,
                       i_shift, i_scale):
    m = m_ref[0, 0]
    y = _normed(x_ref[0], g_ref[...], NORM_EPS)
    h = y * (1.0 + m[i_scale:i_scale + 1]) + m[i_shift:i_shift + 1]
    h_ref[0] = h
    h_hi = h.astype(BF16)
    h_lo = (h - h_hi.astype(F32)).astype(BF16)
    part = (jnp.dot(h_hi, wr_ref[...], preferred_element_type=F32)
            + jnp.dot(h_lo, wr_ref[...], preferred_element_type=F32))
    logits = part + pltpu.roll(part, LANES - N_EXPERTS, axis=1) + br_ref[...]
    lane = lax.broadcasted_iota(jnp.int32, logits.shape, 1)
    logits = jnp.where(lane < N_EXPERTS, logits, NEG_INF)
    vals, ids = [], []
    for _ in range(TOP_K):
        mx = jnp.max(logits, axis=-1, keepdims=True)
        ik = jnp.min(jnp.where(logits == mx, lane, LANES), axis=-1, keepdims=True)
        vals.append(mx)
        ids.append(ik)
        logits = jnp.where(lane == ik, -jnp.inf, logits)
    es = [jnp.exp(v - vals[0]) for v in vals]
    inv = 1.0 / (es[0] + es[1] + es[2] + es[3])
    gate = jnp.zeros(lane.shape, F32)
    for k in range(TOP_K):
        gate = jnp.where(lane == k, es[k] * inv, gate)
    gate_ref[0] = gate

    @pl.when((pl.program_id(0) == 0) & (pl.program_id(1) == 0))
    def _():
        cnt_ref[...] = jnp.zeros(cnt_ref.shape, cnt_ref.dtype)

    uses = jnp.zeros(lane.shape, F32)
    for k in range(TOP_K):
        uses = jnp.where(lane == ids[k], 1.0, uses)
    tr = uses.shape[0]
    row_i = lax.broadcasted_iota(jnp.int32, (tr, tr), 0)
    col_i = lax.broadcasted_iota(jnp.int32, (tr, tr), 1)
    local = jnp.dot(jnp.where(row_i > col_i, 1.0, 0.0).astype(BF16), uses.astype(BF16),
                    preferred_element_type=F32)
    base = cnt_ref[...]
    base_ref[0] = base
    cnt_ref[...] = base + jnp.sum(uses, axis=0, keepdims=True)
    t_loc = lax.broadcasted_iota(jnp.int32, lane.shape, 0).astype(F32)
    col_f = col_i.astype(F32)
    meta = jnp.zeros(lane.shape, jnp.int32)
    grid = jnp.zeros((LANES, tr), F32)
    for k in range(TOP_K):
        mine = lane == ids[k]
        l_rank = jnp.sum(jnp.where(mine, local, 0.0), axis=-1, keepdims=True)
        rank = l_rank + jnp.sum(jnp.where(mine, base, 0.0), axis=-1, keepdims=True)
        meta = jnp.where(lane == k, ids[k], meta)
        meta = jnp.where(lane == TOP_K + k, rank.astype(jnp.int32), meta)
        slot_hot = jnp.where(l_rank == col_f, 1.0, 0.0).astype(BF16)
        who = jnp.transpose(jnp.where(mine, t_loc, 0.0)).astype(BF16)
        grid = grid + jnp.dot(who, slot_hot, preferred_element_type=F32)
    meta_ref[0] = jnp.transpose(meta)[0:2 * TOP_K, :]
    grid_ref[0] = grid[0:N_EXPERTS, :].astype(jnp.int32)


def _rms_router(x, g, mods, w_router, b_router, i_shift, i_scale):
    bsz, nt, d = x.shape
    w_hi = w_router.astype(BF16)
    w_lo = (w_router - w_hi.astype(F32)).astype(BF16)
    wr = jnp.zeros((d, LANES), BF16).at[:, :N_EXPERTS].set(w_hi).at[:, N_EXPERTS:2 * N_EXPERTS].set(w_lo)
    br = jnp.zeros((1, LANES), F32).at[0, :N_EXPERTS].set(b_router)
    row = pl.BlockSpec((1, ROW_TILE, d), lambda b, j: (b, j, 0))
    lanes = pl.BlockSpec((1, ROW_TILE, LANES), lambda b, j: (b, j, 0))
    return pl.pallas_call(
        functools.partial(_rms_router_kernel, i_shift=i_shift, i_scale=i_scale),
        grid=(bsz, nt // ROW_TILE),
        in_specs=[
            row,
            pl.BlockSpec((1, d), lambda b, j: (0, 0)),
            _mods_spec(),
            pl.BlockSpec((d, LANES), lambda b, j: (0, 0)),
            pl.BlockSpec((1, LANES), lambda b, j: (0, 0)),
        ],
        out_specs=[row, lanes,
                   pl.BlockSpec((1, 2 * TOP_K, ROW_TILE), lambda b, j: (b, 0, j)),
                   pl.BlockSpec((1, LANES), lambda b, j: (0, 0)),
                   pl.BlockSpec((1, N_EXPERTS, ROW_TILE), lambda b, j: (b * (nt // ROW_TILE) + j, 0, 0)),
                   pl.BlockSpec((1, 1, LANES), lambda b, j: (b * (nt // ROW_TILE) + j, 0, 0))],
        out_shape=[
            jax.ShapeDtypeStruct((bsz, nt, d), F32),
            jax.ShapeDtypeStruct((bsz, nt, LANES), F32),
            jax.ShapeDtypeStruct((bsz, 2 * TOP_K, nt), jnp.int32),
            jax.ShapeDtypeStruct((1, LANES), F32),
            jax.ShapeDtypeStruct((bsz * nt // ROW_TILE, N_EXPERTS, ROW_TILE), jnp.int32),
            jax.ShapeDtypeStruct((bsz * nt // ROW_TILE, 1, LANES), F32),
        ],
        compiler_params=_params(("arbitrary", "arbitrary")),
        name="rms_router",
    )(x, g.reshape(1, d), mods, wr, br)


def _rope_tables(n_ctx, n_lat):
    half = HEAD_DIM // 2
    nf = half // 2
    t = jnp.arange(n_lat)
    inv_freq = ROPE_BASE ** (-jnp.arange(nf, dtype=F32) / nf)
    lane = jnp.arange(LANES)
    d = lane % HEAD_DIM
    pos = jnp.where((d < half)[None, :], (t // GRID_W)[:, None], (t % GRID_W)[:, None]).astype(F32)
    ang = pos * inv_freq[d % nf][None, :]
    sign = jnp.where((d % half) < nf, -1.0, 1.0)[None, :]
    cos = jnp.concatenate([jnp.ones((n_ctx, LANES), F32), jnp.cos(ang)], axis=0)
    sin = jnp.concatenate([jnp.zeros((n_ctx, LANES), F32), jnp.sin(ang) * sign], axis=0)
    return cos, sin


def _qkv_kernel(x_ref, w_ref, *rest, n_q_chunks, n_rope_chunks, q_scale):
    if n_rope_chunks:
        cos_ref, sin_ref, o_ref = rest
    else:
        (o_ref,) = rest
    acc = jnp.dot(x_ref[0], w_ref[...], preferred_element_type=F32)
    chunks = acc.shape[1] // LANES
    j = pl.program_id(2)
    nf = HEAD_DIM // 4
    for c in range(chunks):
        gc = j * chunks + c
        a = acc[:, c * LANES:(c + 1) * LANES]
        if n_rope_chunks:
            lane = lax.broadcasted_iota(jnp.int32, a.shape, 1)
            swapped = jnp.where((lane % (2 * nf)) < nf,
                                pltpu.roll(a, LANES - nf, axis=1), pltpu.roll(a, nf, axis=1))
            use = gc < n_rope_chunks
            a = a * jnp.where(use, cos_ref[...], 1.0) + swapped * jnp.where(use, sin_ref[...], 0.0)
        a = a * jnp.where(gc < n_q_chunks, q_scale, 1.0)
        o_ref[0, :, c * LANES:(c + 1) * LANES] = a.astype(o_ref.dtype)


def _qkv_proj(h, w, n_q_cols, n_rope_cols, rope, q_scale=HEAD_DIM ** -0.5):
    bsz, nt, d = h.shape
    n = w.shape[1]
    tn = 512
    tm = _mm_row_tile(nt)
    in_specs = [
        pl.BlockSpec((1, tm, d), lambda b, i, j: (b, i, 0)),
        pl.BlockSpec((d, tn), lambda b, i, j: (0, j)),
    ]
    args = [h, w.astype(BF16)]
    if n_rope_cols:
        in_specs += [pl.BlockSpec((tm, LANES), lambda b, i, j: (i, 0))] * 2
        args += list(rope)
    return pl.pallas_call(
        functools.partial(_qkv_kernel, n_q_chunks=n_q_cols // LANES, n_rope_chunks=n_rope_cols // LANES,
                          q_scale=q_scale),
        grid=(bsz, nt // tm, n // tn),
        in_specs=in_specs,
        out_specs=pl.BlockSpec((1, tm, tn), lambda b, i, j: (b, i, j)),
        out_shape=jax.ShapeDtypeStruct((bsz, nt, n), BF16),
        compiler_params=_params(("arbitrary", "arbitrary", "arbitrary")),
        name="qkv_proj",
    )(*args)


def _out_proj_kernel(o_ref, w_ref, x_ref, m_ref, y_ref, *, n_ctx, i_gate):
    acc = jnp.dot(o_ref[0], w_ref[...], preferred_element_type=F32)
    tm = acc.shape[0]
    row = pl.program_id(1) * tm + lax.broadcasted_iota(jnp.int32, (tm, 1), 0)
    m = m_ref[0]
    gate = jnp.where(row < n_ctx, m[0, i_gate:i_gate + 1], m[1, i_gate:i_gate + 1])
    y_ref[0] = x_ref[0] + gate * acc


def _out_proj(o, w, x, mods, n_ctx, i_gate):
    bsz, nt, d = o.shape
    n = w.shape[1]
    tn = 512
    tm = _mm_row_tile(nt)
    return pl.pallas_call(
        functools.partial(_out_proj_kernel, n_ctx=n_ctx, i_gate=i_gate),
        grid=(bsz, nt // tm, n // tn),
        in_specs=[
            pl.BlockSpec((1, tm, d), lambda b, i, j: (b, i, 0)),
            pl.BlockSpec((d, tn), lambda b, i, j: (0, j)),
            pl.BlockSpec((1, tm, tn), lambda b, i, j: (b, i, j)),
            pl.BlockSpec((1, 2, 6, tn), lambda b, i, j: (b, 0, 0, j)),
        ],
        out_specs=pl.BlockSpec((1, tm, tn), lambda b, i, j: (b, i, j)),
        out_shape=jax.ShapeDtypeStruct((bsz, nt, n), F32),
        compiler_params=_params(("arbitrary", "arbitrary", "arbitrary")),
        name="out_proj",
    )(o, w.astype(BF16), x, mods)


def _softmax_pv(s, v, extra=None, base2=False):
    exp = jnp.exp2 if base2 else jnp.exp
    mx = jnp.max(s, axis=-1, keepdims=True)
    if extra is not None:
        mx = jnp.maximum(mx, extra)
    e = exp(s - mx)
    den = jnp.sum(e, axis=-1, keepdims=True)
    if extra is not None:
        den = den + exp(extra - mx)
    return jnp.dot(e.astype(BF16), v, preferred_element_type=F32) / den


def _split_heads(x):
    lo = lax.broadcasted_iota(jnp.int32, x.shape, 1) < HEAD_DIM
    zero = jnp.zeros_like(x)
    return jnp.concatenate([jnp.where(lo, x, zero), jnp.where(lo, zero, x)], axis=0)


def _merge_heads(o):
    n = o.shape[0] // 2
    lo = lax.broadcasted_iota(jnp.int32, (n, o.shape[1]), 1) < HEAD_DIM
    return jnp.where(lo, o[:n], o[n:])


def _na_block_geometry(rows):
    n_blk = rows // NA_QROWS
    r0 = [j * NA_QROWS for j in range(n_blk)]
    ks = [min(max(r - NA_ROWS // 2, 0), rows - NA_WIN) for r in r0]
    cfg = [0 if j == 0 else (2 if j == n_blk - 1 else 1) for j in range(n_blk)]
    return r0, ks, cfg


def _na_bias_table(rpb):
    qc = jnp.arange(GRID_W)[:, None]
    kc = jnp.arange(GRID_W)[None, :]
    win = jnp.clip(qc - NA_COLS // 2, 0, GRID_W - NA_COLS)
    col_ok = (kc >= win) & (kc < win + NA_COLS)
    col_off = jnp.clip(kc - qc + NA_COLS - 1, 0, 2 * NA_COLS - 2)
    pick = (col_off[None] == jnp.arange(2 * NA_COLS - 1)[:, None, None]).astype(F32)
    tab = jnp.einsum("hrd,dqk->hrqk", rpb.astype(F32), pick, precision=lax.Precision.HIGHEST)
    return jnp.where(col_ok[None, None], tab, NEG_INF)


def _na_kernel(q_ref, k_ref, v_ref, bias_ref, o_ref, tab, *, n_ctx, rows):
    nq = NA_QROWS * GRID_W
    nwin = NA_WIN * GRID_W
    n_blk = rows // NA_QROWS

    @pl.when(pl.program_id(1) == 0)
    def _():
        r0s, kss, cfgs = _na_block_geometry(rows)
        assert all(r0s[j] - kss[j] == r0s[1] - kss[1] for j in range(n_blk) if cfgs[j] == 1)
        tab[...] = jnp.full(tab.shape, NEG_INF, tab.dtype)
        for c in range(3):
            j = cfgs.index(c)
            for a in range(NA_QROWS):
                r = r0s[j] + a
                rs = min(max(r - NA_ROWS // 2, 0), rows - NA_ROWS)
                for i in range(NA_WIN):
                    kr = kss[j] + i
                    if rs <= kr < rs + NA_ROWS:
                        for hh in range(2):
                            tab[c, hh * nq + a * GRID_W:hh * nq + (a + 1) * GRID_W, i * GRID_W:(i + 1) * GRID_W] = (
                                bias_ref[hh, kr - r + NA_ROWS - 1])

    s = _dot_nt(_split_heads(q_ref[0, 0:n_ctx, :]), k_ref[0, 0:n_ctx, :])
    o_ref[0, 0:n_ctx, :] = _merge_heads(_softmax_pv(s, v_ref[0, 0:n_ctx, :])).astype(o_ref.dtype)

    def block(j, carry):
        r0 = j * NA_QROWS
        ks = jnp.clip(r0 - NA_ROWS // 2, 0, rows - NA_WIN)
        cfg = jnp.where(j == 0, 0, jnp.where(j == n_blk - 1, 2, 1))
        q0 = pl.multiple_of(n_ctx + r0 * GRID_W, GRID_W)
        k0 = pl.multiple_of(n_ctx + ks * GRID_W, GRID_W)
        kk = jnp.concatenate([k_ref[0, 0:n_ctx, :], k_ref[0, pl.ds(k0, nwin), :]], axis=0)
        vv = jnp.concatenate([v_ref[0, 0:n_ctx, :], v_ref[0, pl.ds(k0, nwin), :]], axis=0)
        s = _dot_nt(_split_heads(q_ref[0, pl.ds(q0, nq), :]), kk)
        s = jnp.concatenate([s[:, :n_ctx], s[:, n_ctx:] + tab[cfg]], axis=1)
        o_ref[0, pl.ds(q0, nq), :] = _merge_heads(_softmax_pv(s, vv)).astype(o_ref.dtype)
        return carry

    lax.fori_loop(0, n_blk, block, 0, unroll=4)


def _na_attention(qkv, bias_tab, n_ctx):
    bsz, nt, _ = qkv.shape
    rows = (nt - n_ctx) // GRID_W
    n_hp = N_HEADS // 2
    blk = lambda off: pl.BlockSpec((1, nt, LANES), lambda p, b: (b, 0, p + off))
    return pl.pallas_call(
        functools.partial(_na_kernel, n_ctx=n_ctx, rows=rows),
        grid=(n_hp, bsz),
        in_specs=[
            blk(0), blk(n_hp), blk(2 * n_hp),
            pl.BlockSpec((2,) + bias_tab.shape[1:], lambda p, b: (p, 0, 0, 0)),
        ],
        out_specs=pl.BlockSpec((1, nt, LANES), lambda p, b: (b, 0, p)),
        out_shape=jax.ShapeDtypeStruct((bsz, nt, D_MODEL), BF16),
        scratch_shapes=[pltpu.VMEM((3, 2 * NA_QROWS * GRID_W, NA_WIN * GRID_W), F32)],
        compiler_params=_params(("arbitrary", "arbitrary")),
        name="na_attention",
    )(qkv, qkv, qkv, bias_tab)


def _sw_kernel(sink_ref, q_ref, k_ref, v_ref, o_ref, *, n_ctx, n_lat):
    hd = HEAD_DIM
    qb = Q_BLOCK
    n_blk = n_lat // qb
    p = pl.program_id(1)
    jl = pl.program_id(2) - n_ctx // qb
    starts, valids = [], []
    for c in range(3):
        blk = jl - 1 + c
        starts.append(pl.multiple_of(n_ctx + qb * jnp.clip(blk, 0, n_blk - 1), qb))
        valids.append((blk >= 0) & (blk < n_blk) & (jl >= 0))
    iq = lax.broadcasted_iota(jnp.int32, (qb, qb), 0)
    ik = lax.broadcasted_iota(jnp.int32, (qb, qb), 1)
    oks = [valids[c] & (jnp.abs(iq - ik - (c - 1) * qb) <= SW_WINDOW) for c in range(3)]
    ok = jnp.concatenate(oks + [jnp.full((qb, n_ctx), True)], axis=1)
    k_all = jnp.concatenate([k_ref[0, pl.ds(s0, qb), :] for s0 in starts] + [k_ref[0, 0:n_ctx, :]], axis=0)
    v_all = jnp.concatenate([v_ref[0, pl.ds(s0, qb), :] for s0 in starts] + [v_ref[0, 0:n_ctx, :]], axis=0)
    gw = GQA_GROUP * hd
    for g in range(2):
        kg = jnp.concatenate([k_all[:, g * hd:(g + 1) * hd]] * 2, axis=1)
        vg = jnp.concatenate([v_all[:, g * hd:(g + 1) * hd]] * 2, axis=1)
        for c0 in range(0, GQA_GROUP // 2, SW_STACK // 2):
            cols = [g * gw + c * LANES for c in range(c0, c0 + SW_STACK // 2)]
            qs = jnp.concatenate([_split_heads(q_ref[0, :, c:c + LANES]) for c in cols], axis=0)
            s = _dot_nt(qs, kg)
            s = jnp.concatenate([jnp.where(ok, s[i * qb:(i + 1) * qb], NEG_INF) for i in range(SW_STACK)], axis=0)
            sink = jnp.concatenate([jnp.full((qb, 1), sink_ref[(p * 2 + g) * GQA_GROUP + 2 * c0 + i], F32)
                                    for i in range(SW_STACK)], axis=0)
            o = _softmax_pv(s, vg, extra=sink)
            for n, c in enumerate(cols):
                o_ref[0, :, c:c + LANES] = _merge_heads(o[2 * n * qb:(2 * n + 2) * qb]).astype(o_ref.dtype)


def _sw_attention(qkv, sinks, n_ctx):
    bsz, nt, _ = qkv.shape
    n_lat = nt - n_ctx
    qw = 2 * GQA_GROUP * HEAD_DIM
    k_blk = D_MODEL // LANES
    v_blk = k_blk + N_KV_HEADS * HEAD_DIM // LANES
    grid_spec = pltpu.PrefetchScalarGridSpec(
        num_scalar_prefetch=1,
        grid=(bsz, N_KV_HEADS // 2, nt // Q_BLOCK),
        in_specs=[
            pl.BlockSpec((1, Q_BLOCK, qw), lambda b, p, j, s: (b, j, p)),
            pl.BlockSpec((1, nt, LANES), lambda b, p, j, s: (b, 0, k_blk + p)),
            pl.BlockSpec((1, nt, LANES), lambda b, p, j, s: (b, 0, v_blk + p)),
        ],
        out_specs=pl.BlockSpec((1, Q_BLOCK, qw), lambda b, p, j, s: (b, j, p)),
    )
    return pl.pallas_call(
        functools.partial(_sw_kernel, n_ctx=n_ctx, n_lat=n_lat),
        grid_spec=grid_spec,
        out_shape=jax.ShapeDtypeStruct((bsz, nt, D_MODEL), BF16),
        compiler_params=_params(("arbitrary", "arbitrary", "arbitrary")),
        name="sw_attention",
    )(sinks.astype(F32), qkv, qkv, qkv)


def _diff_kernel(lam_ref, q_ref, k_ref, v_ref, g_ref, o_ref, *, n_ctx, lambda_init):
    lam = lam_ref[0]
    nt = k_ref.shape[1]
    tq = DIFF_QBLOCK

    def attend(q, n_keys):
        n = q.shape[0]
        qs = _split_heads(q)
        k = k_ref[0, 0:n_keys, :]
        v = v_ref[0, 0:n_keys, :]
        outs = [_softmax_pv(_dot_nt(qs[i * n:(i + 1) * n], k), v, base2=True) for i in range(2)]
        o = outs[0] - lam * outs[1]
        return (_normed(o, g_ref[...], 1e-5) * (1.0 - lambda_init)).astype(o_ref.dtype)

    o_ref[0, 0:n_ctx, :] = attend(q_ref[0, 0:n_ctx, :], n_ctx)

    def block(j, carry):
        r0 = pl.multiple_of(n_ctx + j * tq, LANES)
        o_ref[0, pl.ds(r0, tq), :] = attend(q_ref[0, pl.ds(r0, tq), :], nt)
        return carry

    lax.fori_loop(0, (nt - n_ctx) // tq, block, 0, unroll=4)


def _diff_attention(qkv, lam, sub_g, lambda_init, n_ctx):
    bsz, nt, _ = qkv.shape
    nh = N_DIFF_HEADS
    assert (nt - n_ctx) % DIFF_QBLOCK == 0
    grid_spec = pltpu.PrefetchScalarGridSpec(
        num_scalar_prefetch=1,
        grid=(bsz, nh),
        in_specs=[
            pl.BlockSpec((1, nt, LANES), lambda b, h, s: (b, 0, h)),
            pl.BlockSpec((1, nt, LANES), lambda b, h, s: (b, 0, nh + h)),
            pl.BlockSpec((1, nt, LANES), lambda b, h, s: (b, 0, 2 * nh + h)),
            pl.BlockSpec((1, LANES), lambda b, h, s: (0, 0)),
        ],
        out_specs=pl.BlockSpec((1, nt, LANES), lambda b, h, s: (b, 0, h)),
    )
    return pl.pallas_call(
        functools.partial(_diff_kernel, n_ctx=n_ctx, lambda_init=lambda_init),
        grid_spec=grid_spec,
        out_shape=jax.ShapeDtypeStruct((bsz, nt, D_MODEL), BF16),
        compiler_params=_params(("arbitrary", "arbitrary")),
        name="diff_attention",
    )(lam.reshape(1).astype(F32), qkv, qkv, qkv, sub_g.reshape(1, LANES).astype(F32))


def _route(meta, cnt, grid, base):
    bsz, _, nt = meta.shape
    n_pairs = bsz * nt * TOP_K
    tm = EXPERT_TILE
    n_rows = n_pairs + (N_EXPERTS + FFN_SLOTS - 1) * tm
    n_tiles = n_rows // tm
    experts = jnp.arange(N_EXPERTS, dtype=jnp.int32)
    counts = cnt[0, :N_EXPERTS].astype(jnp.int32)
    padded = (counts + tm - 1) // tm * tm
    gend = jnp.cumsum(padded)
    gstart = gend - padded
    idx, rank = meta[:, :TOP_K, :], meta[:, TOP_K:, :]
    group = jnp.sum(jnp.where(idx[..., None] == experts, gstart, 0), axis=-1)
    pos = (group + rank).transpose(0, 2, 1).reshape(n_pairs).astype(jnp.int32)
    tile_start = jnp.arange(n_tiles, dtype=jnp.int32) * tm
    tile_expert = jnp.minimum(jnp.sum((gend[None, :] <= tile_start[:, None]).astype(jnp.int32), axis=1),
                              N_EXPERTS - 1)
    hot = tile_expert[:, None] == experts
    tile_gstart = jnp.sum(jnp.where(hot, gstart, 0), axis=-1)
    tile_count = jnp.sum(jnp.where(hot, counts, 0), axis=-1)
    tile_rows = jnp.clip(tile_count - (tile_start - tile_gstart), 0, tm).astype(jnp.int32)
    tile_first = (tile_start == tile_gstart).astype(jnp.int32)
    n_tt, _, tr = grid.shape
    starts = base[:, 0, :N_EXPERTS].astype(jnp.int32)
    tile_starts = jnp.sum(jnp.where(hot[:, None, :], starts[None], 0), axis=-1)
    r = (tile_start - tile_gstart)[:, None] + jnp.arange(tm, dtype=jnp.int32)[None, :]
    reached = tile_starts[:, None, :] <= r[:, :, None]
    tok_tile = jnp.clip(jnp.sum(reached.astype(jnp.int32), axis=-1) - 1, 0, n_tt - 1)
    first = jnp.max(jnp.where(reached, tile_starts[:, None, :], 0), axis=-1)
    slot = jnp.clip(r - first, 0, tr - 1)
    flat = (tok_tile * N_EXPERTS + tile_expert[:, None]) * tr + slot
    src = jnp.take(grid.reshape(-1), flat.reshape(-1), axis=0).reshape(n_tiles, tm) + tok_tile * tr
    return pos, src.reshape(n_tiles, 1, tm).astype(jnp.int32), tile_expert.astype(jnp.int32), tile_rows, tile_first


def _gather_copy(h_hbm, buf, sem, slot):
    return pltpu.make_async_copy(h_hbm.at[pl.ds(0, buf.shape[1])], buf.at[slot], sem.at[slot])


def _ffn_kernel(te_ref, tr_ref, tf_ref, src0_ref, src1_ref, src2_ref, h_hbm, wgu_ref, bgu_ref, wd_ref, bd_ref,
                y_ref, buf, sem, wgu_bf, wd_bf):
    i = pl.program_id(0)
    tm = buf.shape[1]
    slot = i % FFN_SLOTS

    def issue(src_ref, dst_slot):
        for r in range(tm):
            pltpu.make_async_copy(h_hbm.at[pl.ds(src_ref[0, 0, r], 1)],
                                  buf.at[dst_slot, pl.ds(r, 1)], sem.at[dst_slot]).start(priority=r % 2)

    @pl.when(i == 0)
    def _():
        issue(src0_ref, 0)
        issue(src1_ref, 1)

    valid = tr_ref[i] > 0
    fetched = (i < 2) | (tr_ref[jnp.maximum(i - 2, 0)] > 0)

    @pl.when(valid & (tf_ref[i] > 0))
    def _():
        wgu_bf[...] = wgu_ref[0].astype(BF16)
        wd_bf[...] = wd_ref[0].astype(BF16)

    for s in range(FFN_SLOTS):
        @pl.when(valid & (slot == s))
        def _():
            _gather_copy(h_hbm, buf, sem, s).wait()
            x = buf[s].astype(BF16)
            gu = jnp.dot(x, wgu_bf[...], preferred_element_type=F32) + bgu_ref[0]
            gate = jnp.minimum(gu[:, :D_EXPERT], SWIGLU_LIMIT)
            up = jnp.clip(gu[:, D_EXPERT:], -SWIGLU_LIMIT, SWIGLU_LIMIT)
            act = (up + 1.0) * (gate * jax.nn.sigmoid(SWIGLU_ALPHA * gate))
            y_ref[...] = jnp.dot(act.astype(BF16), wd_bf[...], preferred_element_type=F32) + bd_ref[0]
            issue(src2_ref, (s + 2) % FFN_SLOTS)

    @pl.when(jnp.logical_not(valid))
    def _():
        y_ref[...] = jnp.zeros(y_ref.shape, y_ref.dtype)

    @pl.when(jnp.logical_not(valid) & fetched)
    def _():
        _gather_copy(h_hbm, buf, sem, slot).wait()


def _expert_ffn(h, src, tile_expert, tile_rows, tile_first, layer, w_gu, b_gu, w_down, b_down):
    n_tok, d = h.shape
    n_tiles, _, tm = src.shape
    n_all = w_gu.shape[0] * N_EXPERTS

    def src_spec(ahead):
        return pl.BlockSpec((1, 1, tm), lambda i, te, tr, tf: (jnp.minimum(i + ahead, n_tiles - 1), 0, 0),
                            memory_space=pltpu.SMEM)

    def w_spec(shape):
        return pl.BlockSpec((1,) + shape, lambda i, te, tr, tf: (layer * N_EXPERTS + te[i], 0, 0))

    grid_spec = pltpu.PrefetchScalarGridSpec(
        num_scalar_prefetch=3,
        grid=(n_tiles,),
        in_specs=[
            src_spec(0), src_spec(1), src_spec(2),
            pl.BlockSpec(memory_space=pl.ANY),
            w_spec((d, 2 * D_EXPERT)), w_spec((1, 2 * D_EXPERT)), w_spec((D_EXPERT, d)), w_spec((1, d)),
        ],
        out_specs=pl.BlockSpec((tm, d), lambda i, te, tr, tf: (i, 0)),
        scratch_shapes=[pltpu.VMEM((FFN_SLOTS, tm, d), F32), pltpu.SemaphoreType.DMA((FFN_SLOTS,)),
                        pltpu.VMEM((d, 2 * D_EXPERT), BF16), pltpu.VMEM((D_EXPERT, d), BF16)],
    )
    return pl.pallas_call(
        _ffn_kernel,
        grid_spec=grid_spec,
        out_shape=jax.ShapeDtypeStruct((n_tiles * tm, d), F32),
        compiler_params=_params(("arbitrary",)),
        name="expert_ffn",
    )(tile_expert, tile_rows, tile_first, src, src, src, h,
      w_gu.reshape(n_all, d, 2 * D_EXPERT), b_gu.reshape(n_all, 1, 2 * D_EXPERT),
      w_down.reshape(n_all, D_EXPERT, d), b_down.reshape(n_all, 1, d))


def _combine_kernel(pos0_ref, pos1_ref, pos2_ref, y_hbm, gate_ref, x_ref, m_ref, g_ref, *rest, i_gate, final):
    if final:
        f_ref, buf, sem = rest
    else:
        nm_ref, o_ref, h_ref, buf, sem = rest
    i = pl.program_id(0)
    n = pl.num_programs(0)
    tc = buf.shape[2]
    slot = i % COMBINE_SLOTS

    def issue(pos_ref, dst_slot):
        for r in range(tc):
            for k in range(TOP_K):
                pltpu.make_async_copy(y_hbm.at[pl.ds(pos_ref[0, 0, r * TOP_K + k], 1)],
                                      buf.at[dst_slot, k, pl.ds(r, 1)], sem.at[dst_slot]).start(priority=r % 2)

    @pl.when(i == 0)
    def _():
        issue(pos0_ref, 0)
        issue(pos1_ref, 1)

    def wait(s):
        for k in range(TOP_K):
            pltpu.make_async_copy(y_hbm.at[pl.ds(0, tc)], buf.at[s, k], sem.at[s]).wait()

    for s in range(COMBINE_SLOTS):
        @pl.when(slot == s)
        def _():
            wait(s)
            g = gate_ref[...]
            y = g[:, 0:1] * buf[s, 0]
            for k in range(1, TOP_K):
                y = y + g[:, k:k + 1] * buf[s, k]
            x_new = x_ref[...] + m_ref[0, 0, i_gate:i_gate + 1] * y
            normed = _normed(x_new, g_ref[...], NORM_EPS)
            if final:
                f_ref[...] = normed
            else:
                o_ref[...] = x_new
                nm = nm_ref[0, 0]
                h_ref[...] = (normed * (1.0 + nm[1:2]) + nm[0:1]).astype(h_ref.dtype)
            issue(pos2_ref, (s + 2) % COMBINE_SLOTS)

    @pl.when(i == n - 1)
    def _():
        wait((slot + 1) % COMBINE_SLOTS)
        wait((slot + 2) % COMBINE_SLOTS)


def _combine(y, pos, gate, x, mods, i_gate, tiles_per_sample, next_g, next_mods):
    n_tok, d = x.shape
    tc = ROW_TILE
    n_steps = n_tok // tc
    pos3 = pos.reshape(n_steps, 1, tc * TOP_K)
    final = next_mods is None
    assert n_steps >= 2

    def mod_map(i):
        return (i // tiles_per_sample, jnp.minimum(i % tiles_per_sample, 1), 0, 0)

    def pos_spec(ahead):
        return pl.BlockSpec((1, 1, tc * TOP_K), lambda i: (jnp.minimum(i + ahead, n_steps - 1), 0, 0),
                            memory_space=pltpu.SMEM)

    def latent_map(i):
        return (i // tiles_per_sample * (tiles_per_sample - 1) + jnp.maximum(i % tiles_per_sample - 1, 0), 0)

    row = pl.BlockSpec((tc, d), lambda i: (i, 0))
    in_specs = [
        pos_spec(0), pos_spec(1), pos_spec(2),
        pl.BlockSpec(memory_space=pl.ANY),
        pl.BlockSpec((tc, LANES), lambda i: (i, 0)),
        row,
        pl.BlockSpec((1, 1, 6, d), mod_map),
        pl.BlockSpec((1, d), lambda i: (0, 0)),
    ]
    args = [pos3, pos3, pos3, y, gate, x, mods, next_g.reshape(1, d)]
    if final:
        n_out = n_tok // tiles_per_sample * (tiles_per_sample - 1)
        out_specs = pl.BlockSpec((tc, d), latent_map)
        out_shape = jax.ShapeDtypeStruct((n_out, d), F32)
    else:
        in_specs.append(pl.BlockSpec((1, 1, 6, d), mod_map))
        args.append(next_mods)
        out_specs = [row, row]
        out_shape = [jax.ShapeDtypeStruct((n_tok, d), F32), jax.ShapeDtypeStruct((n_tok, d), BF16)]
    return pl.pallas_call(
        functools.partial(_combine_kernel, i_gate=i_gate, final=final),
        grid=(n_steps,),
        in_specs=in_specs,
        out_specs=out_specs,
        out_shape=out_shape,
        scratch_shapes=[pltpu.VMEM((COMBINE_SLOTS, TOP_K, tc, d), F32), pltpu.SemaphoreType.DMA((COMBINE_SLOTS,))],
        compiler_params=_params(("arbitrary",)),
        name="moe_combine",
    )(*args)


def _moe_block(x, g, mods, w_router, b_router, layer, w_gu, b_gu, w_down, b_down, next_g, next_mods):
    bsz, nt, d = x.shape
    h, gate, meta, cnt, grid, base = _rms_router(x, g, mods, w_router, b_router, i_shift=3, i_scale=4)
    n_tok = bsz * nt
    pos, src, tile_expert, tile_rows, tile_first = _route(meta, cnt, grid, base)
    y = _expert_ffn(h.reshape(n_tok, d), src, tile_expert, tile_rows, tile_first, layer,
                    w_gu, b_gu, w_down, b_down)
    out = _combine(y, pos, gate.reshape(n_tok, LANES), x.reshape(n_tok, d), mods, 5, nt // ROW_TILE,
                   next_g, next_mods)
    if next_mods is None:
        return out.reshape(bsz, -1, d)
    return out[0].reshape(bsz, nt, d), out[1].reshape(bsz, nt, d)


def kernel(x, c, ctx, c_ctx, ada_w, ada_b, norm_attn_g, norm_ffn_g, na_w_qkv, na_rpb, na_w_o, sw_w_qkv, sw_sinks, sw_w_o, diff_w_qkv, diff_lam_q1, diff_lam_k1, diff_lam_q2, diff_lam_k2, diff_sub_g, diff_w_o, router_w, router_b, exp_w_gu, exp_b_gu, exp_w_down, exp_b_down, final_g):
    bsz, n_lat, d = x.shape
    n_ctx = ctx.shape[1]
    assert d == D_MODEL and n_ctx == CTX_LEN and n_lat % GRID_W == 0 and bsz < ADA_ROWS
    assert n_lat % ROW_TILE == 0 and n_ctx == ROW_TILE

    cond = jnp.concatenate([c, c_ctx[None, :], jnp.zeros((ADA_ROWS - bsz - 1, d), c.dtype)], axis=0)
    ada = _ada_all(jax.nn.silu(cond).astype(BF16), ada_w, ada_b)
    rope = _rope_tables(n_ctx, n_lat)

    xs = jnp.concatenate([ctx, x], axis=1)
    def layer_mods(i):
        m_lat = ada[i, :bsz].reshape(bsz, 6, d)
        m_ctx = jnp.broadcast_to(ada[i, bsz].reshape(1, 6, d), (bsz, 6, d))
        return jnp.stack([m_ctx, m_lat], axis=1)

    h = _rms_mod(xs, norm_attn_g[0], layer_mods(0), i_shift=0, i_scale=1)
    for i in range(DEPTH):
        mods = layer_mods(i)
        j = i // N_MIXERS
        kind = i % N_MIXERS
        if kind == 0:
            qkv = _qkv_proj(h, na_w_qkv[j], d, 0, None)
            o = _na_attention(qkv, _na_bias_table(na_rpb[j]), n_ctx)
            w_o = na_w_o[j]
        elif kind == 1:
            n_kv = N_KV_HEADS * HEAD_DIM
            qkv = _qkv_proj(h, sw_w_qkv[j], d, d + n_kv, rope)
            o = _sw_attention(qkv, sw_sinks[j], n_ctx)
            w_o = sw_w_o[j]
        else:
            lambda_init = 0.8 - 0.6 * math.exp(-0.3 * i)
            lam = (jnp.exp(jnp.sum(diff_lam_q1[j].astype(F32) * diff_lam_k1[j].astype(F32)))
                   - jnp.exp(jnp.sum(diff_lam_q2[j].astype(F32) * diff_lam_k2[j].astype(F32))) + lambda_init)
            qkv = _qkv_proj(h, diff_w_qkv[j], d, 2 * d, rope, q_scale=HEAD_DIM ** -0.5 * LOG2_E)
            o = _diff_attention(qkv, lam, diff_sub_g[j], lambda_init, n_ctx)
            w_o = diff_w_o[j]
        xs = _out_proj(o, w_o, xs, mods, n_ctx, i_gate=2)
        moe_args = (xs, norm_ffn_g[i], mods, router_w[i], router_b[i], i, exp_w_gu, exp_b_gu, exp_w_down, exp_b_down)
        if i + 1 < DEPTH:
            xs, h = _moe_block(*moe_args, norm_attn_g[i + 1], layer_mods(i + 1))
        else:
            out = _moe_block(*moe_args, final_g, None)
    return out
```

```python
import functools
import math

import jax
import jax.numpy as jnp
from jax import lax
from jax.experimental import pallas as pl
from jax.experimental.pallas import tpu as pltpu

D_MODEL = 2048
DEPTH = 4
CTX_LEN = 256
GRID_W = 64
N_MIXERS = 3
HEAD_DIM = 64
N_HEADS = D_MODEL // HEAD_DIM
N_KV_HEADS = N_HEADS // 8
GQA_GROUP = N_HEADS // N_KV_HEADS
N_DIFF_HEADS = N_HEADS // 2
NA_ROWS = 8
NA_COLS = 16
NA_QROWS = 4
NA_WIN = 12
SW_WINDOW = 128
Q_BLOCK = 128
DIFF_QBLOCK = 256
SW_STACK = 4
ROPE_BASE = 10000.0
N_EXPERTS = 32
TOP_K = 4
D_EXPERT = D_MODEL // 4
SWIGLU_LIMIT = 7.0
SWIGLU_ALPHA = 1.702
NORM_EPS = 1e-6
NEG_INF = -1e30
LOG2_E = 1.4426950408889634

LANES = 128
VMEM_LIMIT_BYTES = 52 * 1024 * 1024
ROW_TILE = 256
MM_ROW_TILES = (1088, 1024, 640, 512, 256)
MM_COL_TILES = (1024, 512)
EXPERT_TILE = 256
FFN_SLOTS = 3
COMBINE_SLOTS = 3
ADA_ROWS = 16

F32 = jnp.float32
BF16 = jnp.bfloat16


def _params(semantics):
    return pltpu.CompilerParams(dimension_semantics=semantics, vmem_limit_bytes=VMEM_LIMIT_BYTES)


def _mm_row_tile(nt):
    return next(t for t in MM_ROW_TILES if nt % t == 0)


def _mm_col_tile(n):
    return next(t for t in MM_COL_TILES if n % t == 0)


def _dot_nt(a, b):
    return lax.dot_general(a, b, (((1,), (1,)), ((), ())), preferred_element_type=F32)


def _ada_kernel(s_ref, w_ref, b_ref, o_ref):
    o_ref[0] = jnp.dot(s_ref[...], w_ref[0].astype(BF16), preferred_element_type=F32) + b_ref[0]


def _ada_all(s, ada_w, ada_b):
    depth, d, n = ada_w.shape
    tn = 1024
    return pl.pallas_call(
        _ada_kernel,
        grid=(depth, n // tn),
        in_specs=[
            pl.BlockSpec((ADA_ROWS, d), lambda i, j: (0, 0)),
            pl.BlockSpec((1, d, tn), lambda i, j: (i, 0, j)),
            pl.BlockSpec((1, 1, tn), lambda i, j: (i, 0, j)),
        ],
        out_specs=pl.BlockSpec((1, ADA_ROWS, tn), lambda i, j: (i, 0, j)),
        out_shape=jax.ShapeDtypeStruct((depth, ADA_ROWS, n), F32),
        compiler_params=_params(("arbitrary", "arbitrary")),
        name="ada_ln",
    )(s, ada_w, ada_b.reshape(depth, 1, n))


def _normed(x, g, eps):
    ms = jnp.mean(x * x, axis=-1, keepdims=True)
    return x * lax.rsqrt(ms + eps) * g


def _rms_mod_kernel(x_ref, g_ref, m_ref, o_ref, *, i_shift, i_scale):
    m = m_ref[0, 0]
    y = _normed(x_ref[0], g_ref[...], NORM_EPS)
    o_ref[0] = (y * (1.0 + m[i_scale:i_scale + 1]) + m[i_shift:i_shift + 1]).astype(o_ref.dtype)


def _mods_spec():
    return pl.BlockSpec((1, 1, 6, D_MODEL), lambda b, j: (b, jnp.minimum(j, 1), 0, 0))


def _rms_mod(x, g, mods, i_shift, i_scale):
    bsz, nt, d = x.shape
    return pl.pallas_call(
        functools.partial(_rms_mod_kernel, i_shift=i_shift, i_scale=i_scale),
        grid=(bsz, nt // ROW_TILE),
        in_specs=[
            pl.BlockSpec((1, ROW_TILE, d), lambda b, j: (b, j, 0)),
            pl.BlockSpec((1, d), lambda b, j: (0, 0)),
            _mods_spec(),
        ],
        out_specs=pl.BlockSpec((1, ROW_TILE, d), lambda b, j: (b, j, 0)),
        out_shape=jax.ShapeDtypeStruct((bsz, nt, d), BF16),
        compiler_params=_params(("arbitrary", "arbitrary")),
        name="rms_mod",
    )(x, g.reshape(1, d), mods)


def _rms_router_kernel(x_ref, g_ref, m_ref, wr_ref, br_ref, h_ref, gate_ref, meta_ref, cnt_ref, grid_ref, base_ref, *,
                       i_shift, i_scale):
    m = m_ref[0, 0]
    y = _normed(x_ref[0], g_ref[...], NORM_EPS)
    h = y * (1.0 + m[i_scale:i_scale + 1]) + m[i_shift:i_shift + 1]
    h_ref[0] = h
    h_hi = h.astype(BF16)
    h_lo = (h - h_hi.astype(F32)).astype(BF16)
    part = (jnp.dot(h_hi, wr_ref[...], preferred_element_type=F32)
            + jnp.dot(h_lo, wr_ref[...], preferred_element_type=F32))
    logits = part + pltpu.roll(part, LANES - N_EXPERTS, axis=1) + br_ref[...]
    lane = lax.broadcasted_iota(jnp.int32, logits.shape, 1)
    logits = jnp.where(lane < N_EXPERTS, logits, NEG_INF)
    vals, ids = [], []
    for _ in range(TOP_K):
        mx = jnp.max(logits, axis=-1, keepdims=True)
        ik = jnp.min(jnp.where(logits == mx, lane, LANES), axis=-1, keepdims=True)
        vals.append(mx)
        ids.append(ik)
        logits = jnp.where(lane == ik, -jnp.inf, logits)
    es = [jnp.exp(v - vals[0]) for v in vals]
    inv = 1.0 / (es[0] + es[1] + es[2] + es[3])
    gate = jnp.zeros(lane.shape, F32)
    for k in range(TOP_K):
        gate = jnp.where(lane == k, es[k] * inv, gate)
    gate_ref[0] = gate

    @pl.when((pl.program_id(0) == 0) & (pl.program_id(1) == 0))
    def _():
        cnt_ref[...] = jnp.zeros(cnt_ref.shape, cnt_ref.dtype)

    uses = jnp.zeros(lane.shape, F32)
    for k in range(TOP_K):
        uses = jnp.where(lane == ids[k], 1.0, uses)
    tr = uses.shape[0]
    row_i = lax.broadcasted_iota(jnp.int32, (tr, tr), 0)
    col_i = lax.broadcasted_iota(jnp.int32, (tr, tr), 1)
    local = jnp.dot(jnp.where(row_i > col_i, 1.0, 0.0).astype(BF16), uses.astype(BF16),
                    preferred_element_type=F32)
    base = cnt_ref[...]
    base_ref[0] = base
    cnt_ref[...] = base + jnp.sum(uses, axis=0, keepdims=True)
    t_loc = lax.broadcasted_iota(jnp.int32, lane.shape, 0).astype(F32)
    col_f = col_i.astype(F32)
    meta = jnp.zeros(lane.shape, jnp.int32)
    grid = jnp.zeros((LANES, tr), F32)
    for k in range(TOP_K):
        mine = lane == ids[k]
        l_rank = jnp.sum(jnp.where(mine, local, 0.0), axis=-1, keepdims=True)
        rank = l_rank + jnp.sum(jnp.where(mine, base, 0.0), axis=-1, keepdims=True)
        meta = jnp.where(lane == k, ids[k], meta)
        meta = jnp.where(lane == TOP_K + k, rank.astype(jnp.int32), meta)
        slot_hot = jnp.where(l_rank == col_f, 1.0, 0.0).astype(BF16)
        who = jnp.transpose(jnp.where(mine, t_loc, 0.0)).astype(BF16)
        grid = grid + jnp.dot(who, slot_hot, preferred_element_type=F32)
    meta_ref[0] = jnp.transpose(meta)[0:2 * TOP_K, :]
    grid_ref[0] = grid[0:N_EXPERTS, :].astype(jnp.int32)


def _rms_router(x, g, mods, w_router, b_router, i_shift, i_scale):
    bsz, nt, d = x.shape
    w_hi = w_router.astype(BF16)
    w_lo = (w_router - w_hi.astype(F32)).astype(BF16)
    wr = jnp.zeros((d, LANES), BF16).at[:, :N_EXPERTS].set(w_hi).at[:, N_EXPERTS:2 * N_EXPERTS].set(w_lo)
    br = jnp.zeros((1, LANES), F32).at[0, :N_EXPERTS].set(b_router)
    row = pl.BlockSpec((1, ROW_TILE, d), lambda b, j: (b, j, 0))
    lanes = pl.BlockSpec((1, ROW_TILE, LANES), lambda b, j: (b, j, 0))
    return pl.pallas_call(
        functools.partial(_rms_router_kernel, i_shift=i_shift, i_scale=i_scale),
        grid=(bsz, nt // ROW_TILE),
        in_specs=[
            row,
            pl.BlockSpec((1, d), lambda b, j: (0, 0)),
            _mods_spec(),
            pl.BlockSpec((d, LANES), lambda b, j: (0, 0)),
            pl.BlockSpec((1, LANES), lambda b, j: (0, 0)),
        ],
        out_specs=[row, lanes,
                   pl.BlockSpec((1, 2 * TOP_K, ROW_TILE), lambda b, j: (b, 0, j)),
                   pl.BlockSpec((1, LANES), lambda b, j: (0, 0)),
                   pl.BlockSpec((1, N_EXPERTS, ROW_TILE), lambda b, j: (b * (nt // ROW_TILE) + j, 0, 0)),
                   pl.BlockSpec((1, 1, LANES), lambda b, j: (b * (nt // ROW_TILE) + j, 0, 0))],
        out_shape=[
            jax.ShapeDtypeStruct((bsz, nt, d), F32),
            jax.ShapeDtypeStruct((bsz, nt, LANES), F32),
            jax.ShapeDtypeStruct((bsz, 2 * TOP_K, nt), jnp.int32),
            jax.ShapeDtypeStruct((1, LANES), F32),
            jax.ShapeDtypeStruct((bsz * nt // ROW_TILE, N_EXPERTS, ROW_TILE), jnp.int32),
            jax.ShapeDtypeStruct((bsz * nt // ROW_TILE, 1, LANES), F32),
        ],
        compiler_params=_params(("arbitrary", "arbitrary")),
        name="rms_router",
    )(x, g.reshape(1, d), mods, wr, br)


def _rope_tables(n_ctx, n_lat):
    half = HEAD_DIM // 2
    nf = half // 2
    t = jnp.arange(n_lat)
    inv_freq = ROPE_BASE ** (-jnp.arange(nf, dtype=F32) / nf)
    lane = jnp.arange(LANES)
    d = lane % HEAD_DIM
    pos = jnp.where((d < half)[None, :], (t // GRID_W)[:, None], (t % GRID_W)[:, None]).astype(F32)
    ang = pos * inv_freq[d % nf][None, :]
    sign = jnp.where((d % half) < nf, -1.0, 1.0)[None, :]
    cos = jnp.concatenate([jnp.ones((n_ctx, LANES), F32), jnp.cos(ang)], axis=0)
    sin = jnp.concatenate([jnp.zeros((n_ctx, LANES), F32), jnp.sin(ang) * sign], axis=0)
    return cos, sin


def _qkv_kernel(x_ref, w_ref, *rest, n_q_chunks, n_rope_chunks, q_scale):
    if n_rope_chunks:
        cos_ref, sin_ref, o_ref = rest
    else:
        (o_ref,) = rest
    acc = jnp.dot(x_ref[0], w_ref[...], preferred_element_type=F32)
    chunks = acc.shape[1] // LANES
    j = pl.program_id(2)
    nf = HEAD_DIM // 4
    for c in range(chunks):
        gc = j * chunks + c
        a = acc[:, c * LANES:(c + 1) * LANES]
        if n_rope_chunks:
            lane = lax.broadcasted_iota(jnp.int32, a.shape, 1)
            swapped = jnp.where((lane % (2 * nf)) < nf,
                                pltpu.roll(a, LANES - nf, axis=1), pltpu.roll(a, nf, axis=1))
            use = gc < n_rope_chunks
            a = a * jnp.where(use, cos_ref[...], 1.0) + swapped * jnp.where(use, sin_ref[...], 0.0)
        a = a * jnp.where(gc < n_q_chunks, q_scale, 1.0)
        o_ref[0, :, c * LANES:(c + 1) * LANES] = a.astype(o_ref.dtype)


def _qkv_proj(h, w, n_q_cols, n_rope_cols, rope, q_scale=HEAD_DIM ** -0.5):
    bsz, nt, d = h.shape
    n = w.shape[1]
    tn = _mm_col_tile(n)
    tm = _mm_row_tile(nt)
    in_specs = [
        pl.BlockSpec((1, tm, d), lambda b, i, j: (b, i, 0)),
        pl.BlockSpec((d, tn), lambda b, i, j: (0, j)),
    ]
    args = [h, w.astype(BF16)]
    if n_rope_cols:
        in_specs += [pl.BlockSpec((tm, LANES), lambda b, i, j: (i, 0))] * 2
        args += list(rope)
    return pl.pallas_call(
        functools.partial(_qkv_kernel, n_q_chunks=n_q_cols // LANES, n_rope_chunks=n_rope_cols // LANES,
                          q_scale=q_scale),
        grid=(bsz, nt // tm, n // tn),
        in_specs=in_specs,
        out_specs=pl.BlockSpec((1, tm, tn), lambda b, i, j: (b, i, j)),
        out_shape=jax.ShapeDtypeStruct((bsz, nt, n), BF16),
        compiler_params=_params(("arbitrary", "arbitrary", "arbitrary")),
        name="qkv_proj",
    )(*args)


def _out_proj_kernel(o_ref, w_ref, x_ref, m_ref, y_ref, *, n_ctx, i_gate):
    acc = jnp.dot(o_ref[0], w_ref[...], preferred_element_type=F32)
    tm = acc.shape[0]
    row = pl.program_id(1) * tm + lax.broadcasted_iota(jnp.int32, (tm, 1), 0)
    m = m_ref[0]
    gate = jnp.where(row < n_ctx, m[0, i_gate:i_gate + 1], m[1, i_gate:i_gate + 1])
    y_ref[0] = x_ref[0] + gate * acc


def _out_proj(o, w, x, mods, n_ctx, i_gate):
    bsz, nt, d = o.shape
    n = w.shape[1]
    tn = _mm_col_tile(n)
    tm = _mm_row_tile(nt)
    return pl.pallas_call(
        functools.partial(_out_proj_kernel, n_ctx=n_ctx, i_gate=i_gate),
        grid=(bsz, nt // tm, n // tn),
        in_specs=[
            pl.BlockSpec((1, tm, d), lambda b, i, j: (b, i, 0)),
            pl.BlockSpec((d, tn), lambda b, i, j: (0, j)),
            pl.BlockSpec((1, tm, tn), lambda b, i, j: (b, i, j)),
            pl.BlockSpec((1, 2, 6, tn), lambda b, i, j: (b, 0, 0, j)),
        ],
        out_specs=pl.BlockSpec((1, tm, tn), lambda b, i, j: (b, i, j)),
        out_shape=jax.ShapeDtypeStruct((bsz, nt, n), F32),
        compiler_params=_params(("arbitrary", "arbitrary", "arbitrary")),
        name="out_proj",
    )(o, w.astype(BF16), x, mods)


def _softmax_pv(s, v, extra=None, base2=False):
    exp = jnp.exp2 if base2 else jnp.exp
    mx = jnp.max(s, axis=-1, keepdims=True)
    if extra is not None:
        mx = jnp.maximum(mx, extra)
    e = exp(s - mx)
    den = jnp.sum(e, axis=-1, keepdims=True)
    if extra is not None:
        den = den + exp(extra - mx)
    return jnp.dot(e.astype(BF16), v, preferred_element_type=F32) / den


def _split_heads(x):
    lo = lax.broadcasted_iota(jnp.int32, x.shape, 1) < HEAD_DIM
    zero = jnp.zeros_like(x)
    return jnp.concatenate([jnp.where(lo, x, zero), jnp.where(lo, zero, x)], axis=0)


def _merge_heads(o):
    n = o.shape[0] // 2
    lo = lax.broadcasted_iota(jnp.int32, (n, o.shape[1]), 1) < HEAD_DIM
    return jnp.where(lo, o[:n], o[n:])


def _na_block_geometry(rows):
    n_blk = rows // NA_QROWS
    r0 = [j * NA_QROWS for j in range(n_blk)]
    ks = [min(max(r - NA_ROWS // 2, 0), rows - NA_WIN) for r in r0]
    cfg = [0 if j == 0 else (2 if j == n_blk - 1 else 1) for j in range(n_blk)]
    return r0, ks, cfg


def _na_bias_table(rpb):
    qc = jnp.arange(GRID_W)[:, None]
    kc = jnp.arange(GRID_W)[None, :]
    win = jnp.clip(qc - NA_COLS // 2, 0, GRID_W - NA_COLS)
    col_ok = (kc >= win) & (kc < win + NA_COLS)
    col_off = jnp.clip(kc - qc + NA_COLS - 1, 0, 2 * NA_COLS - 2)
    pick = (col_off[None] == jnp.arange(2 * NA_COLS - 1)[:, None, None]).astype(F32)
    tab = jnp.einsum("hrd,dqk->hrqk", rpb.astype(F32), pick, precision=lax.Precision.HIGHEST)
    return jnp.where(col_ok[None, None], tab, NEG_INF)


def _na_kernel(q_ref, k_ref, v_ref, bias_ref, o_ref, tab, *, n_ctx, rows):
    nq = NA_QROWS * GRID_W
    nwin = NA_WIN * GRID_W
    n_blk = rows // NA_QROWS

    @pl.when(pl.program_id(1) == 0)
    def _():
        r0s, kss, cfgs = _na_block_geometry(rows)
        assert all(r0s[j] - kss[j] == r0s[1] - kss[1] for j in range(n_blk) if cfgs[j] == 1)
        tab[...] = jnp.full(tab.shape, NEG_INF, tab.dtype)
        for c in range(3):
            j = cfgs.index(c)
            for a in range(NA_QROWS):
                r = r0s[j] + a
                rs = min(max(r - NA_ROWS // 2, 0), rows - NA_ROWS)
                for i in range(NA_WIN):
                    kr = kss[j] + i
                    if rs <= kr < rs + NA_ROWS:
                        for hh in range(2):
                            tab[c, hh * nq + a * GRID_W:hh * nq + (a + 1) * GRID_W, i * GRID_W:(i + 1) * GRID_W] = (
                                bias_ref[hh, kr - r + NA_ROWS - 1])

    s = _dot_nt(_split_heads(q_ref[0, 0:n_ctx, :]), k_ref[0, 0:n_ctx, :])
    o_ref[0, 0:n_ctx, :] = _merge_heads(_softmax_pv(s, v_ref[0, 0:n_ctx, :])).astype(o_ref.dtype)

    def block(j, carry):
        r0 = j * NA_QROWS
        ks = jnp.clip(r0 - NA_ROWS // 2, 0, rows - NA_WIN)
        cfg = jnp.where(j == 0, 0, jnp.where(j == n_blk - 1, 2, 1))
        q0 = pl.multiple_of(n_ctx + r0 * GRID_W, GRID_W)
        k0 = pl.multiple_of(n_ctx + ks * GRID_W, GRID_W)
        kk = jnp.concatenate([k_ref[0, 0:n_ctx, :], k_ref[0, pl.ds(k0, nwin), :]], axis=0)
        vv = jnp.concatenate([v_ref[0, 0:n_ctx, :], v_ref[0, pl.ds(k0, nwin), :]], axis=0)
        s = _dot_nt(_split_heads(q_ref[0, pl.ds(q0, nq), :]), kk)
        s = jnp.concatenate([s[:, :n_ctx], s[:, n_ctx:] + tab[cfg]], axis=1)
        o_ref[0, pl.ds(q0, nq), :] = _merge_heads(_softmax_pv(s, vv)).astype(o_ref.dtype)
        return carry

    lax.fori_loop(0, n_blk, block, 0, unroll=8)


def _na_attention(qkv, bias_tab, n_ctx):
    bsz, nt, _ = qkv.shape
    rows = (nt - n_ctx) // GRID_W
    n_hp = N_HEADS // 2
    blk = lambda off: pl.BlockSpec((1, nt, LANES), lambda p, b: (b, 0, p + off))
    return pl.pallas_call(
        functools.partial(_na_kernel, n_ctx=n_ctx, rows=rows),
        grid=(n_hp, bsz),
        in_specs=[
            blk(0), blk(n_hp), blk(2 * n_hp),
            pl.BlockSpec((2,) + bias_tab.shape[1:], lambda p, b: (p, 0, 0, 0)),
        ],
        out_specs=pl.BlockSpec((1, nt, LANES), lambda p, b: (b, 0, p)),
        out_shape=jax.ShapeDtypeStruct((bsz, nt, D_MODEL), BF16),
        scratch_shapes=[pltpu.VMEM((3, 2 * NA_QROWS * GRID_W, NA_WIN * GRID_W), F32)],
        compiler_params=_params(("arbitrary", "arbitrary")),
        name="na_attention",
    )(qkv, qkv, qkv, bias_tab)


def _sw_kernel(sink_ref, q_ref, k_ref, v_ref, o_ref, *, n_ctx, n_lat):
    hd = HEAD_DIM
    qb = Q_BLOCK
    n_blk = n_lat // qb
    p = pl.program_id(1)
    jl = pl.program_id(2) - n_ctx // qb
    starts, valids = [], []
    for c in range(3):
        blk = jl - 1 + c
        starts.append(pl.multiple_of(n_ctx + qb * jnp.clip(blk, 0, n_blk - 1), qb))
        valids.append((blk >= 0) & (blk < n_blk) & (jl >= 0))
    iq = lax.broadcasted_iota(jnp.int32, (qb, qb), 0)
    ik = lax.broadcasted_iota(jnp.int32, (qb, qb), 1)
    oks = [valids[c] & (jnp.abs(iq - ik - (c - 1) * qb) <= SW_WINDOW) for c in range(3)]
    ok = jnp.concatenate(oks + [jnp.full((qb, n_ctx), True)], axis=1)
    k_all = jnp.concatenate([k_ref[0, pl.ds(s0, qb), :] for s0 in starts] + [k_ref[0, 0:n_ctx, :]], axis=0)
    v_all = jnp.concatenate([v_ref[0, pl.ds(s0, qb), :] for s0 in starts] + [v_ref[0, 0:n_ctx, :]], axis=0)
    gw = GQA_GROUP * hd
    for g in range(2):
        kg = jnp.concatenate([k_all[:, g * hd:(g + 1) * hd]] * 2, axis=1)
        vg = jnp.concatenate([v_all[:, g * hd:(g + 1) * hd]] * 2, axis=1)
        for c0 in range(0, GQA_GROUP // 2, SW_STACK // 2):
            cols = [g * gw + c * LANES for c in range(c0, c0 + SW_STACK // 2)]
            qs = jnp.concatenate([_split_heads(q_ref[0, :, c:c + LANES]) for c in cols], axis=0)
            s = _dot_nt(qs, kg)
            s = jnp.concatenate([jnp.where(ok, s[i * qb:(i + 1) * qb], NEG_INF) for i in range(SW_STACK)], axis=0)
            sink = jnp.concatenate([jnp.full((qb, 1), sink_ref[(p * 2 + g) * GQA_GROUP + 2 * c0 + i], F32)
                                    for i in range(SW_STACK)], axis=0)
            o = _softmax_pv(s, vg, extra=sink)
            for n, c in enumerate(cols):
                o_ref[0, :, c:c + LANES] = _merge_heads(o[2 * n * qb:(2 * n + 2) * qb]).astype(o_ref.dtype)


def _sw_attention(qkv, sinks, n_ctx):
    bsz, nt, _ = qkv.shape
    n_lat = nt - n_ctx
    qw = 2 * GQA_GROUP * HEAD_DIM
    k_blk = D_MODEL // LANES
    v_blk = k_blk + N_KV_HEADS * HEAD_DIM // LANES
    grid_spec = pltpu.PrefetchScalarGridSpec(
        num_scalar_prefetch=1,
        grid=(bsz, N_KV_HEADS // 2, nt // Q_BLOCK),
        in_specs=[
            pl.BlockSpec((1, Q_BLOCK, qw), lambda b, p, j, s: (b, j, p)),
            pl.BlockSpec((1, nt, LANES), lambda b, p, j, s: (b, 0, k_blk + p)),
            pl.BlockSpec((1, nt, LANES), lambda b, p, j, s: (b, 0, v_blk + p)),
        ],
        out_specs=pl.BlockSpec((1, Q_BLOCK, qw), lambda b, p, j, s: (b, j, p)),
    )
    return pl.pallas_call(
        functools.partial(_sw_kernel, n_ctx=n_ctx, n_lat=n_lat),
        grid_spec=grid_spec,
        out_shape=jax.ShapeDtypeStruct((bsz, nt, D_MODEL), BF16),
        compiler_params=_params(("arbitrary", "arbitrary", "arbitrary")),
        name="sw_attention",
    )(sinks.astype(F32), qkv, qkv, qkv)


def _diff_kernel(lam_ref, q_ref, k_ref, v_ref, g_ref, o_ref, *, n_ctx, lambda_init):
    lam = lam_ref[0]
    nt = k_ref.shape[1]
    tq = DIFF_QBLOCK

    def attend(q, n_keys):
        n = q.shape[0]
        qs = _split_heads(q)
        k = k_ref[0, 0:n_keys, :]
        v = v_ref[0, 0:n_keys, :]
        outs = [_softmax_pv(_dot_nt(qs[i * n:(i + 1) * n], k), v, base2=True) for i in range(2)]
        o = outs[0] - lam * outs[1]
        return (_normed(o, g_ref[...], 1e-5) * (1.0 - lambda_init)).astype(o_ref.dtype)

    o_ref[0, 0:n_ctx, :] = attend(q_ref[0, 0:n_ctx, :], n_ctx)

    def block(j, carry):
        r0 = pl.multiple_of(n_ctx + j * tq, LANES)
        o_ref[0, pl.ds(r0, tq), :] = attend(q_ref[0, pl.ds(r0, tq), :], nt)
        return carry

    lax.fori_loop(0, (nt - n_ctx) // tq, block, 0, unroll=4)


def _diff_attention(qkv, lam, sub_g, lambda_init, n_ctx):
    bsz, nt, _ = qkv.shape
    nh = N_DIFF_HEADS
    assert (nt - n_ctx) % DIFF_QBLOCK == 0
    grid_spec = pltpu.PrefetchScalarGridSpec(
        num_scalar_prefetch=1,
        grid=(bsz, nh),
        in_specs=[
            pl.BlockSpec((1, nt, LANES), lambda b, h, s: (b, 0, h)),
            pl.BlockSpec((1, nt, LANES), lambda b, h, s: (b, 0, nh + h)),
            pl.BlockSpec((1, nt, LANES), lambda b, h, s: (b, 0, 2 * nh + h)),
            pl.BlockSpec((1, LANES), lambda b, h, s: (0, 0)),
        ],
        out_specs=pl.BlockSpec((1, nt, LANES), lambda b, h, s: (b, 0, h)),
    )
    return pl.pallas_call(
        functools.partial(_diff_kernel, n_ctx=n_ctx, lambda_init=lambda_init),
        grid_spec=grid_spec,
        out_shape=jax.ShapeDtypeStruct((bsz, nt, D_MODEL), BF16),
        compiler_params=_params(("arbitrary", "arbitrary")),
        name="diff_attention",
    )(lam.reshape(1).astype(F32), qkv, qkv, qkv, sub_g.reshape(1, LANES).astype(F32))


def _route(meta, cnt, grid, base):
    bsz, _, nt = meta.shape
    n_pairs = bsz * nt * TOP_K
    tm = EXPERT_TILE
    n_rows = n_pairs + (N_EXPERTS + FFN_SLOTS - 1) * tm
    n_tiles = n_rows // tm
    experts = jnp.arange(N_EXPERTS, dtype=jnp.int32)
    counts = cnt[0, :N_EXPERTS].astype(jnp.int32)
    padded = (counts + tm - 1) // tm * tm
    gend = jnp.cumsum(padded)
    gstart = gend - padded
    idx, rank = meta[:, :TOP_K, :], meta[:, TOP_K:, :]
    group = jnp.sum(jnp.where(idx[..., None] == experts, gstart, 0), axis=-1)
    pos = (group + rank).transpose(0, 2, 1).reshape(n_pairs).astype(jnp.int32)
    tile_start = jnp.arange(n_tiles, dtype=jnp.int32) * tm
    tile_expert = jnp.minimum(jnp.sum((gend[None, :] <= tile_start[:, None]).astype(jnp.int32), axis=1),
                              N_EXPERTS - 1)
    hot = tile_expert[:, None] == experts
    tile_gstart = jnp.sum(jnp.where(hot, gstart, 0), axis=-1)
    tile_count = jnp.sum(jnp.where(hot, counts, 0), axis=-1)
    tile_rows = jnp.clip(tile_count - (tile_start - tile_gstart), 0, tm).astype(jnp.int32)
    tile_first = (tile_start == tile_gstart).astype(jnp.int32)
    n_tt, _, tr = grid.shape
    starts = base[:, 0, :N_EXPERTS].astype(jnp.int32)
    tile_starts = jnp.sum(jnp.where(hot[:, None, :], starts[None], 0), axis=-1)
    r = (tile_start - tile_gstart)[:, None] + jnp.arange(tm, dtype=jnp.int32)[None, :]
    reached = tile_starts[:, None, :] <= r[:, :, None]
    tok_tile = jnp.clip(jnp.sum(reached.astype(jnp.int32), axis=-1) - 1, 0, n_tt - 1)
    first = jnp.max(jnp.where(reached, tile_starts[:, None, :], 0), axis=-1)
    slot = jnp.clip(r - first, 0, tr - 1)
    flat = (tok_tile * N_EXPERTS + tile_expert[:, None]) * tr + slot
    src = jnp.take(grid.reshape(-1), flat.reshape(-1), axis=0).reshape(n_tiles, tm) + tok_tile * tr
    return pos, src.reshape(n_tiles, 1, tm).astype(jnp.int32), tile_expert.astype(jnp.int32), tile_rows, tile_first


def _gather_copy(h_hbm, buf, sem, slot):
    return pltpu.make_async_copy(h_hbm.at[pl.ds(0, buf.shape[1])], buf.at[slot], sem.at[slot])


def _ffn_kernel(te_ref, tr_ref, tf_ref, src0_ref, src1_ref, src2_ref, h_hbm, wgu_ref, bgu_ref, wd_ref, bd_ref,
                y_ref, buf, sem, wgu_bf, wd_bf):
    i = pl.program_id(0)
    tm = buf.shape[1]
    slot = i % FFN_SLOTS

    def issue(src_ref, dst_slot):
        for r in range(tm):
            pltpu.make_async_copy(h_hbm.at[pl.ds(src_ref[0, 0, r], 1)],
                                  buf.at[dst_slot, pl.ds(r, 1)], sem.at[dst_slot]).start(priority=r % 2)

    @pl.when(i == 0)
    def _():
        issue(src0_ref, 0)
        issue(src1_ref, 1)

    valid = tr_ref[i] > 0
    fetched = (i < 2) | (tr_ref[jnp.maximum(i - 2, 0)] > 0)

    @pl.when(valid & (tf_ref[i] > 0))
    def _():
        wgu_bf[...] = wgu_ref[0].astype(BF16)
        wd_bf[...] = wd_ref[0].astype(BF16)

    for s in range(FFN_SLOTS):
        @pl.when(valid & (slot == s))
        def _():
            _gather_copy(h_hbm, buf, sem, s).wait()
            x = buf[s].astype(BF16)
            gu = jnp.dot(x, wgu_bf[...], preferred_element_type=F32) + bgu_ref[0]
            gate = jnp.minimum(gu[:, :D_EXPERT], SWIGLU_LIMIT)
            up = jnp.clip(gu[:, D_EXPERT:], -SWIGLU_LIMIT, SWIGLU_LIMIT)
            act = (up + 1.0) * (gate * jax.nn.sigmoid(SWIGLU_ALPHA * gate))
            y_ref[...] = jnp.dot(act.astype(BF16), wd_bf[...], preferred_element_type=F32) + bd_ref[0]
            issue(src2_ref, (s + 2) % FFN_SLOTS)

    @pl.when(jnp.logical_not(valid))
    def _():
        y_ref[...] = jnp.zeros(y_ref.shape, y_ref.dtype)

    @pl.when(jnp.logical_not(valid) & fetched)
    def _():
        _gather_copy(h_hbm, buf, sem, slot).wait()


def _expert_ffn(h, src, tile_expert, tile_rows, tile_first, layer, w_gu, b_gu, w_down, b_down):
    n_tok, d = h.shape
    n_tiles, _, tm = src.shape
    n_all = w_gu.shape[0] * N_EXPERTS

    def src_spec(ahead):
        return pl.BlockSpec((1, 1, tm), lambda i, te, tr, tf: (jnp.minimum(i + ahead, n_tiles - 1), 0, 0),
                            memory_space=pltpu.SMEM)

    def w_spec(shape):
        return pl.BlockSpec((1,) + shape, lambda i, te, tr, tf: (layer * N_EXPERTS + te[i], 0, 0))

    grid_spec = pltpu.PrefetchScalarGridSpec(
        num_scalar_prefetch=3,
        grid=(n_tiles,),
        in_specs=[
            src_spec(0), src_spec(1), src_spec(2),
            pl.BlockSpec(memory_space=pl.ANY),
            w_spec((d, 2 * D_EXPERT)), w_spec((1, 2 * D_EXPERT)), w_spec((D_EXPERT, d)), w_spec((1, d)),
        ],
        out_specs=pl.BlockSpec((tm, d), lambda i, te, tr, tf: (i, 0)),
        scratch_shapes=[pltpu.VMEM((FFN_SLOTS, tm, d), F32), pltpu.SemaphoreType.DMA((FFN_SLOTS,)),
                        pltpu.VMEM((d, 2 * D_EXPERT), BF16), pltpu.VMEM((D_EXPERT, d), BF16)],
    )
    return pl.pallas_call(
        _ffn_kernel,
        grid_spec=grid_spec,
        out_shape=jax.ShapeDtypeStruct((n_tiles * tm, d), F32),
        compiler_params=_params(("arbitrary",)),
        name="expert_ffn",
    )(tile_expert, tile_rows, tile_first, src, src, src, h,
      w_gu.reshape(n_all, d, 2 * D_EXPERT), b_gu.reshape(n_all, 1, 2 * D_EXPERT),
      w_down.reshape(n_all, D_EXPERT, d), b_down.reshape(n_all, 1, d))


def _combine_kernel(pos0_ref, pos1_ref, pos2_ref, y_hbm, gate_ref, x_ref, m_ref, g_ref, *rest, i_gate, final):
    if final:
        f_ref, buf, sem = rest
    else:
        nm_ref, o_ref, h_ref, buf, sem = rest
    i = pl.program_id(0)
    n = pl.num_programs(0)
    tc = buf.shape[2]
    slot = i % COMBINE_SLOTS

    def issue(pos_ref, dst_slot):
        for r in range(tc):
            for k in range(TOP_K):
                pltpu.make_async_copy(y_hbm.at[pl.ds(pos_ref[0, 0, r * TOP_K + k], 1)],
                                      buf.at[dst_slot, k, pl.ds(r, 1)], sem.at[dst_slot]).start(priority=r % 2)

    @pl.when(i == 0)
    def _():
        issue(pos0_ref, 0)
        issue(pos1_ref, 1)

    def wait(s):
        for k in range(TOP_K):
            pltpu.make_async_copy(y_hbm.at[pl.ds(0, tc)], buf.at[s, k], sem.at[s]).wait()

    for s in range(COMBINE_SLOTS):
        @pl.when(slot == s)
        def _():
            wait(s)
            g = gate_ref[...]
            y = g[:, 0:1] * buf[s, 0]
            for k in range(1, TOP_K):
                y = y + g[:, k:k + 1] * buf[s, k]
            x_new = x_ref[...] + m_ref[0, 0, i_gate:i_gate + 1] * y
            normed = _normed(x_new, g_ref[...], NORM_EPS)
            if final:
                f_ref[...] = normed
            else:
                o_ref[...] = x_new
                nm = nm_ref[0, 0]
                h_ref[...] = (normed * (1.0 + nm[1:2]) + nm[0:1]).astype(h_ref.dtype)
            issue(pos2_ref, (s + 2) % COMBINE_SLOTS)

    @pl.when(i == n - 1)
    def _():
        wait((slot + 1) % COMBINE_SLOTS)
        wait((slot + 2) % COMBINE_SLOTS)


def _combine(y, pos, gate, x, mods, i_gate, tiles_per_sample, next_g, next_mods):
    n_tok, d = x.shape
    tc = ROW_TILE
    n_steps = n_tok // tc
    pos3 = pos.reshape(n_steps, 1, tc * TOP_K)
    final = next_mods is None
    assert n_steps >= 2

    def mod_map(i):
        return (i // tiles_per_sample, jnp.minimum(i % tiles_per_sample, 1), 0, 0)

    def pos_spec(ahead):
        return pl.BlockSpec((1, 1, tc * TOP_K), lambda i: (jnp.minimum(i + ahead, n_steps - 1), 0, 0),
                            memory_space=pltpu.SMEM)

    def latent_map(i):
        return (i // tiles_per_sample * (tiles_per_sample - 1) + jnp.maximum(i % tiles_per_sample - 1, 0), 0)

    row = pl.BlockSpec((tc, d), lambda i: (i, 0))
    in_specs = [
        pos_spec(0), pos_spec(1), pos_spec(2),
        pl.BlockSpec(memory_space=pl.ANY),
        pl.BlockSpec((tc, LANES), lambda i: (i, 0)),
        row,
        pl.BlockSpec((1, 1, 6, d), mod_map),
        pl.BlockSpec((1, d), lambda i: (0, 0)),
    ]
    args = [pos3, pos3, pos3, y, gate, x, mods, next_g.reshape(1, d)]
    if final:
        n_out = n_tok // tiles_per_sample * (tiles_per_sample - 1)
        out_specs = pl.BlockSpec((tc, d), latent_map)
        out_shape = jax.ShapeDtypeStruct((n_out, d), F32)
    else:
        in_specs.append(pl.BlockSpec((1, 1, 6, d), mod_map))
        args.append(next_mods)
        out_specs = [row, row]
        out_shape = [jax.ShapeDtypeStruct((n_tok, d), F32), jax.ShapeDtypeStruct((n_tok, d), BF16)]
    return pl.pallas_call(
        functools.partial(_combine_kernel, i_gate=i_gate, final=final),
        grid=(n_steps,),
        in_specs=in_specs,
        out_specs=out_specs,
        out_shape=out_shape,
        scratch_shapes=[pltpu.VMEM((COMBINE_SLOTS, TOP_K, tc, d), F32), pltpu.SemaphoreType.DMA((COMBINE_SLOTS,))],
        compiler_params=_params(("arbitrary",)),
        name="moe_combine",
    )(*args)


def _moe_block(x, g, mods, w_router, b_router, layer, w_gu, b_gu, w_down, b_down, next_g, next_mods):
    bsz, nt, d = x.shape
    h, gate, meta, cnt, grid, base = _rms_router(x, g, mods, w_router, b_router, i_shift=3, i_scale=4)
    n_tok = bsz * nt
    pos, src, tile_expert, tile_rows, tile_first = _route(meta, cnt, grid, base)
    y = _expert_ffn(h.reshape(n_tok, d), src, tile_expert, tile_rows, tile_first, layer,
                    w_gu, b_gu, w_down, b_down)
    out = _combine(y, pos, gate.reshape(n_tok, LANES), x.reshape(n_tok, d), mods, 5, nt // ROW_TILE,
                   next_g, next_mods)
    if next_mods is None:
        return out.reshape(bsz, -1, d)
    return out[0].reshape(bsz, nt, d), out[1].reshape(bsz, nt, d)


def kernel(x, c, ctx, c_ctx, ada_w, ada_b, norm_attn_g, norm_ffn_g, na_w_qkv, na_rpb, na_w_o, sw_w_qkv, sw_sinks, sw_w_o, diff_w_qkv, diff_lam_q1, diff_lam_k1, diff_lam_q2, diff_lam_k2, diff_sub_g, diff_w_o, router_w, router_b, exp_w_gu, exp_b_gu, exp_w_down, exp_b_down, final_g):
    bsz, n_lat, d = x.shape
    n_ctx = ctx.shape[1]
    assert d == D_MODEL and n_ctx == CTX_LEN and n_lat % GRID_W == 0 and bsz < ADA_ROWS
    assert n_lat % ROW_TILE == 0 and n_ctx == ROW_TILE

    cond = jnp.concatenate([c, c_ctx[None, :], jnp.zeros((ADA_ROWS - bsz - 1, d), c.dtype)], axis=0)
    ada = _ada_all(jax.nn.silu(cond).astype(BF16), ada_w, ada_b)
    rope = _rope_tables(n_ctx, n_lat)

    xs = jnp.concatenate([ctx, x], axis=1)
    def layer_mods(i):
        m_lat = ada[i, :bsz].reshape(bsz, 6, d)
        m_ctx = jnp.broadcast_to(ada[i, bsz].reshape(1, 6, d), (bsz, 6, d))
        return jnp.stack([m_ctx, m_lat], axis=1)

    h = _rms_mod(xs, norm_attn_g[0], layer_mods(0), i_shift=0, i_scale=1)
    for i in range(DEPTH):
        mods = layer_mods(i)
        j = i // N_MIXERS
        kind = i % N_MIXERS
        if kind == 0:
            qkv = _qkv_proj(h, na_w_qkv[j], d, 0, None)
            o = _na_attention(qkv, _na_bias_table(na_rpb[j]), n_ctx)
            w_o = na_w_o[j]
        elif kind == 1:
            n_kv = N_KV_HEADS * HEAD_DIM
            qkv = _qkv_proj(h, sw_w_qkv[j], d, d + n_kv, rope)
            o = _sw_attention(qkv, sw_sinks[j], n_ctx)
            w_o = sw_w_o[j]
        else:
            lambda_init = 0.8 - 0.6 * math.exp(-0.3 * i)
            lam = (jnp.exp(jnp.sum(diff_lam_q1[j].astype(F32) * diff_lam_k1[j].astype(F32)))
                   - jnp.exp(jnp.sum(diff_lam_q2[j].astype(F32) * diff_lam_k2[j].astype(F32))) + lambda_init)
            qkv = _qkv_proj(h, diff_w_qkv[j], d, 2 * d, rope, q_scale=HEAD_DIM ** -0.5 * LOG2_E)
            o = _diff_attention(qkv, lam, diff_sub_g[j], lambda_init, n_ctx)
            w_o = diff_w_o[j]
        xs = _out_proj(o, w_o, xs, mods, n_ctx, i_gate=2)
        moe_args = (xs, norm_ffn_g[i], mods, router_w[i], router_b[i], i, exp_w_gu, exp_b_gu, exp_w_down, exp_b_down)
        if i + 1 < DEPTH:
            xs, h = _moe_block(*moe_args, norm_attn_g[i + 1], layer_mods(i + 1))
        else:
            out = _moe_block(*moe_args, final_g, None)
    return out
```
